```python
import math
import jax, jax.numpy as jnp
from jax import lax
import numpy as np

D_MODEL = 2048
BATCH = 4
SEQ = 2048
DEPTH = 1
DEC_BATCH = 32
DEC_SEQ = 8
PAST_LEN = 8192
PAGE_SIZE = 128

N_META = 16
N_HEADS = 8
HEAD_DIM = 128
ATTN_WIDTH = N_HEADS * HEAD_DIM
IDX_HEADS = 16
IDX_DIM = 64
TOPK_MAX = 256
Q_BLOCK = 128
REL_BUCKETS = 32
REL_MAX_DIST = 128
CONV_CH = 1024
CONV_WIDTH = 31
CONV_CTX = CONV_WIDTH - 1
FF_RAW = -(-8 * D_MODEL // 3)
D_FF = -(-FF_RAW // 256) * 256
D_IN = 3 * ATTN_WIDTH + IDX_HEADS * IDX_DIM + IDX_DIM + IDX_HEADS + 2 * CONV_CH + 2 * D_MODEL
NORM_EPS = 1e-6
MASK_VALUE = -1e30

kernel_name = 'hybrid_dsa_conformer_decode_step'


def rmsnorm(x, g):
    xf = x.astype(jnp.float32)
    y = xf * lax.rsqrt(jnp.mean(xf * xf, axis=-1, keepdims=True) + NORM_EPS)
    return (y * g.astype(jnp.float32)).astype(x.dtype)


def layernorm(x, g, b):
    xf = x.astype(jnp.float32)
    mu = jnp.mean(xf, axis=-1, keepdims=True)
    var = jnp.mean(jnp.square(xf - mu), axis=-1, keepdims=True)
    y = (xf - mu) * lax.rsqrt(var + NORM_EPS)
    return (y * g.astype(jnp.float32) + b.astype(jnp.float32)).astype(x.dtype)


def t5_bucket(rel):
    n = jnp.maximum(rel, 0)
    max_exact = REL_BUCKETS // 2
    nf = jnp.maximum(n, 1).astype(jnp.float32)
    large = max_exact + (jnp.log(nf / max_exact) / math.log(REL_MAX_DIST / max_exact)
                         * (REL_BUCKETS - max_exact)).astype(jnp.int32)
    large = jnp.minimum(large, REL_BUCKETS - 1)
    return jnp.where(n < max_exact, n, large)


def take_rows(a, idx):
    return jax.vmap(lambda ab, ib: ab[ib])(a, idx)


def index_scores(iq, iw, ik, q_pos, k_pos):
    s = jnp.einsum('bqhd,bkd->bqhk', iq, ik).astype(jnp.float32) * (IDX_DIM ** -0.5)
    s = jnp.einsum('bqhk,bqh->bqk', jax.nn.relu(s), iw.astype(jnp.float32)) * (IDX_HEADS ** -0.5)
    return jnp.where(k_pos[None, None, :] <= q_pos[None, :, None], s, -jnp.inf)


def sparse_attend(q, k_sel, v_sel, q_pos, k_pos_sel, rel_bias):
    logits = jnp.einsum('bqhd,bqkhd->bhqk', q, k_sel).astype(jnp.float32) * (HEAD_DIM ** -0.5)
    rel = q_pos[None, :, None] - k_pos_sel
    bias = jnp.moveaxis(rel_bias.astype(jnp.float32)[t5_bucket(rel)], -1, 1)
    logits = jnp.where((rel >= 0)[:, None], logits + bias, MASK_VALUE)
    p = jax.nn.softmax(logits, axis=-1).astype(v_sel.dtype)
    return jnp.einsum('bhqk,bqkhd->bqhd', p, v_sel)


def mixer_inputs(x, g_norm, w_in, q_g, k_g):
    B, T = x.shape[0], x.shape[1]
    xn = rmsnorm(x, g_norm)
    z = xn @ w_in
    sizes = (ATTN_WIDTH, ATTN_WIDTH, ATTN_WIDTH, IDX_HEADS * IDX_DIM, IDX_DIM, IDX_HEADS,
             2 * CONV_CH, D_MODEL, D_MODEL)
    q, k, v, iq, ik, iw, cz, ga, gb = jnp.split(z, np.cumsum(sizes)[:-1].tolist(), axis=-1)
    q = rmsnorm(q.reshape(B, T, N_HEADS, HEAD_DIM), q_g)
    k = rmsnorm(k.reshape(B, T, N_HEADS, HEAD_DIM), k_g)
    v = v.reshape(B, T, N_HEADS, HEAD_DIM)
    iq = iq.reshape(B, T, IDX_HEADS, IDX_DIM)
    u = cz[..., :CONV_CH] * jax.nn.sigmoid(cz[..., CONV_CH:])
    return q, k, v, iq, ik, iw, u, ga, gb


def conv_branch(u, ctx, w_dw, b_dw, ln_g, ln_b, w_b):
    full = jnp.concatenate([ctx.astype(u.dtype), u], axis=1)
    y = lax.conv_general_dilated(full, w_dw[:, None, :], window_strides=(1,), padding='VALID',
                                 dimension_numbers=('NWC', 'WIO', 'NWC'),
                                 feature_group_count=CONV_CH) + b_dw
    y = jax.nn.silu(layernorm(y, ln_g, ln_b))
    return y @ w_b, full[:, -CONV_CTX:]


def merge_and_ffn(x, attn_o, conv_o, ga, gb, w_a, w_out, g_ffn, w_gate, w_up, w_down):
    B, T = x.shape[0], x.shape[1]
    a = attn_o.reshape(B, T, ATTN_WIDTH) @ w_a
    m = jax.nn.sigmoid(ga) * a + jax.nn.sigmoid(gb) * conv_o
    h = x + m @ w_out
    hn = rmsnorm(h, g_ffn)
    return h + (jax.nn.silu(hn @ w_gate) * (hn @ w_up)) @ w_down


def prompt_attention(q, k, v, iq, iw, ik, rel_bias):
    B, T = q.shape[0], q.shape[1]
    n_blk = -(-T // Q_BLOCK)
    pad = n_blk * Q_BLOCK - T
    top_k = min(TOPK_MAX, SEQ // 4)
    k_pos = jnp.arange(T, dtype=jnp.int32)

    def blocks(a):
        a = jnp.pad(a, [(0, 0), (0, pad)] + [(0, 0)] * (a.ndim - 2))
        return jnp.moveaxis(a.reshape((B, n_blk, Q_BLOCK) + a.shape[2:]), 1, 0)

    q_pos_b = jnp.arange(n_blk * Q_BLOCK, dtype=jnp.int32).reshape(n_blk, Q_BLOCK)

    def one_block(args):
        qb, iqb, iwb, qp = args
        s = index_scores(iqb, iwb, ik, qp, k_pos)
        _, sel = lax.top_k(s, top_k)
        return sparse_attend(qb, take_rows(k, sel), take_rows(v, sel), qp, sel, rel_bias)

    o = lax.map(one_block, (blocks(q), blocks(iq), blocks(iw), q_pos_b))
    return jnp.moveaxis(o, 0, 1).reshape(B, n_blk * Q_BLOCK, N_HEADS, HEAD_DIM)[:, :T]


def sample_attention(q, k_new, v_new, iq, iw, ik_new, cache_k, cache_v, cache_ik, page_table, rel_bias):
    B, Q = q.shape[0], q.shape[1]
    L = PAST_LEN + Q
    top_k = min(TOPK_MAX, L // 4)
    ik_past = cache_ik[page_table].reshape(B, PAST_LEN, IDX_DIM)
    ik_all = jnp.concatenate([ik_past.astype(ik_new.dtype), ik_new], axis=1)
    q_pos = PAST_LEN + jnp.arange(Q, dtype=jnp.int32)
    s = index_scores(iq, iw, ik_all, q_pos, jnp.arange(L, dtype=jnp.int32))
    _, sel = lax.top_k(s, top_k)
    in_past = (sel < PAST_LEN)[..., None, None]
    ps = jnp.minimum(sel, PAST_LEN - 1)
    phys = take_rows(page_table, ps // PAGE_SIZE)
    slot = ps % PAGE_SIZE
    ns = jnp.clip(sel - PAST_LEN, 0, Q - 1)
    k_sel = jnp.where(in_past, cache_k[phys, slot].astype(k_new.dtype), take_rows(k_new, ns))
    v_sel = jnp.where(in_past, cache_v[phys, slot].astype(v_new.dtype), take_rows(v_new, ns))
    return sparse_attend(q, k_sel, v_sel, q_pos, sel, rel_bias)


def setup_inputs(seed: int = 0) -> dict:
    key = jax.random.key(seed)
    ks = jax.random.split(key, 26)
    n_pages = PAST_LEN // PAGE_SIZE
    n_used = DEC_BATCH * n_pages
    n_pool = (n_used * 5) // 4

    def nrm(k, shape, scale):
        return jax.random.normal(k, shape, jnp.float32) * scale

    def gain(k, shape):
        return 1.0 + 0.01 * jax.random.normal(k, shape, jnp.float32)

    page_table = jax.random.permutation(ks[6], n_pool)[:n_used].reshape(DEC_BATCH, n_pages).astype(jnp.int32)
    return {
        'x_prompt': nrm(ks[0], (BATCH, SEQ, D_MODEL), 1.0),
        'x_sample': nrm(ks[1], (DEC_BATCH, DEC_SEQ, D_MODEL), 1.0),
        'cache_k': nrm(ks[2], (DEPTH, n_pool, PAGE_SIZE, N_HEADS, HEAD_DIM), 1.0),
        'cache_v': nrm(ks[3], (DEPTH, n_pool, PAGE_SIZE, N_HEADS, HEAD_DIM), 1.0),
        'cache_ik': nrm(ks[4], (DEPTH, n_pool, PAGE_SIZE, IDX_DIM), 1.0),
        'state_conv': nrm(ks[5], (DEPTH, DEC_BATCH, CONV_CTX, CONV_CH), 0.5),
        'page_table': page_table,
        'meta_tokens': nrm(ks[7], (N_META, D_MODEL), 1.0),
        'rel_bias': nrm(ks[8], (REL_BUCKETS, N_HEADS), 0.5),
        'norm_mix_g': gain(ks[9], (DEPTH, D_MODEL)),
        'w_in': nrm(ks[10], (DEPTH, D_MODEL, D_IN), D_MODEL ** -0.5),
        'q_norm_g': gain(ks[11], (DEPTH, HEAD_DIM)),
        'k_norm_g': gain(ks[12], (DEPTH, HEAD_DIM)),
        'w_attn_br': nrm(ks[13], (DEPTH, ATTN_WIDTH, D_MODEL), ATTN_WIDTH ** -0.5),
        'w_dw': nrm(ks[14], (DEPTH, CONV_WIDTH, CONV_CH), CONV_WIDTH ** -0.5),
        'b_dw': nrm(ks[15], (DEPTH, CONV_CH), 0.01),
        'ln_conv_g': gain(ks[16], (DEPTH, CONV_CH)),
        'ln_conv_b': nrm(ks[17], (DEPTH, CONV_CH), 0.01),
        'w_conv_br': nrm(ks[18], (DEPTH, CONV_CH, D_MODEL), CONV_CH ** -0.5),
        'w_out': nrm(ks[19], (DEPTH, D_MODEL, D_MODEL), D_MODEL ** -0.5),
        'norm_ffn_g': gain(ks[20], (DEPTH, D_MODEL)),
        'w_ffn_gate': nrm(ks[21], (DEPTH, D_MODEL, D_FF), D_MODEL ** -0.5),
        'w_ffn_up': nrm(ks[22], (DEPTH, D_MODEL, D_FF), D_MODEL ** -0.5),
        'w_ffn_down': nrm(ks[23], (DEPTH, D_FF, D_MODEL), D_FF ** -0.5),
    }


def reference(x_prompt, x_sample, cache_k, cache_v, cache_ik, state_conv, page_table,
              meta_tokens, rel_bias, norm_mix_g, w_in, q_norm_g, k_norm_g, w_attn_br,
              w_dw, b_dw, ln_conv_g, ln_conv_b, w_conv_br, w_out, norm_ffn_g,
              w_ffn_gate, w_ffn_up, w_ffn_down):
    B = x_prompt.shape[0]
    hp = jnp.concatenate([jnp.broadcast_to(meta_tokens[None].astype(x_prompt.dtype),
                                           (B, N_META, D_MODEL)), x_prompt], axis=1)
    hs = x_sample
    kp_l, vp_l, ikp_l, cp_l = [], [], [], []
    ks_l, vs_l, iks_l, cs_l = [], [], [], []
    for l in range(DEPTH):
        q, k, v, iq, ik, iw, u, ga, gb = mixer_inputs(hp, norm_mix_g[l], w_in[l], q_norm_g[l], k_norm_g[l])
        ao = prompt_attention(q, k, v, iq, iw, ik, rel_bias)
        ctx0 = jnp.zeros((B, CONV_CTX, CONV_CH), u.dtype)
        co, cst = conv_branch(u, ctx0, w_dw[l], b_dw[l], ln_conv_g[l], ln_conv_b[l], w_conv_br[l])
        hp = merge_and_ffn(hp, ao, co, ga, gb, w_attn_br[l], w_out[l], norm_ffn_g[l],
                           w_ffn_gate[l], w_ffn_up[l], w_ffn_down[l])
        kp_l.append(k); vp_l.append(v); ikp_l.append(ik); cp_l.append(cst)
        q, k, v, iq, ik, iw, u, ga, gb = mixer_inputs(hs, norm_mix_g[l], w_in[l], q_norm_g[l], k_norm_g[l])
        ao = sample_attention(q, k, v, iq, iw, ik, cache_k[l], cache_v[l], cache_ik[l], page_table, rel_bias)
        co, cst = conv_branch(u, state_conv[l], w_dw[l], b_dw[l], ln_conv_g[l], ln_conv_b[l], w_conv_br[l])
        hs = merge_and_ffn(hs, ao, co, ga, gb, w_attn_br[l], w_out[l], norm_ffn_g[l],
                           w_ffn_gate[l], w_ffn_up[l], w_ffn_down[l])
        ks_l.append(k); vs_l.append(v); iks_l.append(ik); cs_l.append(cst)
    return (hp[:, N_META:], hs,
            jnp.stack(kp_l), jnp.stack(vp_l), jnp.stack(ikp_l), jnp.stack(cp_l),
            jnp.stack(ks_l), jnp.stack(vs_l), jnp.stack(iks_l), jnp.stack(cs_l))
```

```python
import functools
import math

import numpy as np
import jax
import jax.numpy as jnp
from jax import lax
from jax.experimental import pallas as pl
from jax.experimental.pallas import tpu as pltpu

N_META = 16
N_HEADS = 8
HEAD_DIM = 128
ATTN_WIDTH = N_HEADS * HEAD_DIM
IDX_HEADS = 16
IDX_DIM = 64
IDX_WIDTH = IDX_HEADS * IDX_DIM
TOPK_MAX = 256
REL_BUCKETS = 32
REL_MAX_DIST = 128
CONV_WIDTH = 31
CONV_CTX = CONV_WIDTH - 1
NORM_EPS = 1e-6
MASK_VALUE = -1e30

LANES = 128
SUBLANES = 8
TILE = LANES
FRONT_PAD = TILE - N_META
HALO = 32
INT_MIN = -2 ** 31
VMEM_LIMIT = 52 * 1024 * 1024

F32 = jnp.float32
BF16 = jnp.bfloat16
I32 = jnp.int32
NT_DIMS = (((1,), (1,)), ((), ()))


def _row_block(rows, cap):
    best = None
    for d in range(SUBLANES, min(rows, cap) + 1, SUBLANES):
        if rows % d == 0:
            best = d
    assert best is not None, rows
    return best


def _params(sem):
    return pltpu.CompilerParams(dimension_semantics=sem, vmem_limit_bytes=VMEM_LIMIT)


def _rms(x, g):
    ms = jnp.mean(x * x, axis=-1, keepdims=True)
    return x * lax.rsqrt(ms + NORM_EPS) * g


def _sortable_key(s):
    s = jnp.where(s == 0.0, 0.0, s)
    bits = lax.bitcast_convert_type(s, I32)
    return jnp.where(bits < 0, bits ^ jnp.int32(0x7FFFFFFF), bits)


def _inproj_body(h_ref, g_ref, wq_ref, wk_ref, wv_ref, wiq_ref, wa_ref, wb_ref, wikw_ref, qg_ref, kg_ref,
                 q_ref, kf_ref, kb_ref, vf_ref, vb_ref, iq_ref, u_ref, ikw_ref, xn_ref, *, tn):
    j = pl.program_id(1)

    @pl.when(j == 0)
    def _():
        xn_ref[...] = _rms(h_ref[...], g_ref[...]).astype(BF16)
        ikw_ref[...] = jnp.dot(xn_ref[...], wikw_ref[...], preferred_element_type=F32)

    xn = xn_ref[...]

    def mm(w_ref):
        return jnp.dot(xn, w_ref[...], preferred_element_type=F32)

    def head_norm(z, g):
        cols = [_rms(z[:, c * HEAD_DIM:(c + 1) * HEAD_DIM], g) for c in range(tn // HEAD_DIM)]
        return jnp.concatenate(cols, axis=-1)

    q_ref[...] = head_norm(mm(wq_ref), qg_ref[...]).astype(BF16)
    k = head_norm(mm(wk_ref), kg_ref[...])
    kf_ref[...] = k
    kb_ref[...] = k.astype(BF16)
    v = mm(wv_ref)
    vf_ref[...] = v
    vb_ref[...] = v.astype(BF16)
    iq_ref[...] = mm(wiq_ref).astype(BF16)
    u_ref[...] = mm(wa_ref) * jax.nn.sigmoid(mm(wb_ref))


def _inproj(h2d, g, w, qg, kg):
    rows, d = h2d.shape
    c = w["ca"].shape[1]
    assert c == ATTN_WIDTH, "fused input projection assumes CONV_CH == ATTN_WIDTH"
    tm = _row_block(rows, 640)
    tn = 256
    nj = ATTN_WIDTH // tn
    row_spec = lambda width: pl.BlockSpec((tm, width), lambda i, j: (i, j))
    w_spec = pl.BlockSpec((d, tn), lambda i, j: (0, j))
    const = lambda shape: pl.BlockSpec(shape, lambda i, j: (0, 0))
    out_shape = (
        jax.ShapeDtypeStruct((rows, ATTN_WIDTH), BF16),
        jax.ShapeDtypeStruct((rows, ATTN_WIDTH), F32),
        jax.ShapeDtypeStruct((rows, ATTN_WIDTH), BF16),
        jax.ShapeDtypeStruct((rows, ATTN_WIDTH), F32),
        jax.ShapeDtypeStruct((rows, ATTN_WIDTH), BF16),
        jax.ShapeDtypeStruct((rows, IDX_WIDTH), BF16),
        jax.ShapeDtypeStruct((rows, c), F32),
        jax.ShapeDtypeStruct((rows, LANES), F32),
    )
    return pl.pallas_call(
        functools.partial(_inproj_body, tn=tn),
        grid=(rows // tm, nj),
        in_specs=[pl.BlockSpec((tm, d), lambda i, j: (i, 0)), const((1, d)),
                  w_spec, w_spec, w_spec, w_spec, w_spec, w_spec, const((d, LANES)),
                  const((1, HEAD_DIM)), const((1, HEAD_DIM))],
        out_specs=[row_spec(tn)] * 7 + [pl.BlockSpec((tm, LANES), lambda i, j: (i, 0))],
        out_shape=out_shape,
        scratch_shapes=[pltpu.VMEM((tm, d), BF16)],
        compiler_params=_params(("parallel", "arbitrary")),
        name="inproj",
    )(h2d, g, w["q"], w["k"], w["v"], w["iq"], w["ca"], w["cb"], w["ikw"], qg, kg)


def _conv_body(x_ref, w_ref, b_ref, lg_ref, lb_ref, o_ref, buf_ref, y_ref, *, tt, c):
    i = pl.program_id(1)

    @pl.when(i == 0)
    def _():
        buf_ref[0:HALO, :] = jnp.zeros((HALO, c), F32)

    buf_ref[HALO:HALO + tt, :] = x_ref[0]
    lead = HALO - CONV_CTX
    rsum = jnp.zeros((tt, LANES), F32)
    for cc in range(c // LANES):
        cs = slice(cc * LANES, (cc + 1) * LANES)
        acc = jnp.zeros((tt, LANES), F32) + b_ref[:, cs]
        for j in range(CONV_WIDTH):
            acc = acc + w_ref[j:j + 1, cs] * buf_ref[lead + j:lead + j + tt, cs]
        y_ref[:, cs] = acc
        rsum = rsum + acc
    mu = jnp.sum(rsum, axis=-1, keepdims=True) * (1.0 / c)
    vsum = jnp.zeros((tt, LANES), F32)
    for cc in range(c // LANES):
        cs = slice(cc * LANES, (cc + 1) * LANES)
        dlt = y_ref[:, cs] - mu
        vsum = vsum + dlt * dlt
    inv = lax.rsqrt(jnp.sum(vsum, axis=-1, keepdims=True) * (1.0 / c) + NORM_EPS)
    for cc in range(c // LANES):
        cs = slice(cc * LANES, (cc + 1) * LANES)
        z = (y_ref[:, cs] - mu) * inv * lg_ref[:, cs] + lb_ref[:, cs]
        o_ref[0, :, cs] = (z * jax.nn.sigmoid(z)).astype(BF16)
    buf_ref[0:HALO, :] = buf_ref[tt:tt + HALO, :]


def _conv(full, w_dw, b_dw, ln_g, ln_b):
    n, t, c = full.shape
    tt = _row_block(t, 128)
    assert tt >= HALO, (t, tt)
    const = lambda shape: pl.BlockSpec(shape, lambda b, i: (0, 0))
    blk = pl.BlockSpec((1, tt, c), lambda b, i: (b, i, 0))
    return pl.pallas_call(
        functools.partial(_conv_body, tt=tt, c=c),
        grid=(n, t // tt),
        in_specs=[blk, const((CONV_WIDTH, c)), const((1, c)), const((1, c)), const((1, c))],
        out_specs=blk,
        out_shape=jax.ShapeDtypeStruct((n, t, c), BF16),
        scratch_shapes=[pltpu.VMEM((HALO + tt, c), F32), pltpu.VMEM((tt, c), F32)],
        compiler_params=_params(("parallel", "arbitrary")),
        name="conv",
    )(full, w_dw, b_dw, ln_g, ln_b)


def _gate_body(h_ref, g_ref, ao_ref, yc_ref, wga_ref, wgb_ref, wa_ref, wcb_ref, m_ref, xn_ref):
    @pl.when(pl.program_id(1) == 0)
    def _():
        xn_ref[...] = _rms(h_ref[...], g_ref[...]).astype(BF16)

    xn = xn_ref[...]
    ga = jnp.dot(xn, wga_ref[...], preferred_element_type=F32)
    gb = jnp.dot(xn, wgb_ref[...], preferred_element_type=F32)
    a = jnp.dot(ao_ref[...], wa_ref[...], preferred_element_type=F32)
    co = jnp.dot(yc_ref[...], wcb_ref[...], preferred_element_type=F32)
    m_ref[...] = (jax.nn.sigmoid(ga) * a + jax.nn.sigmoid(gb) * co).astype(BF16)


def _gate(h2d, g, ao, yc, wga, wgb, wa, wcb):
    rows, d = h2d.shape
    tm = _row_block(rows, 640)
    tn = 256
    full_rows = lambda width: pl.BlockSpec((tm, width), lambda i, j: (i, 0))
    wcol = lambda k: pl.BlockSpec((k, tn), lambda i, j: (0, j))
    return pl.pallas_call(
        _gate_body,
        grid=(rows // tm, d // tn),
        in_specs=[full_rows(d), pl.BlockSpec((1, d), lambda i, j: (0, 0)),
                  full_rows(ao.shape[1]), full_rows(yc.shape[1]),
                  wcol(d), wcol(d), wcol(wa.shape[0]), wcol(wcb.shape[0])],
        out_specs=pl.BlockSpec((tm, tn), lambda i, j: (i, j)),
        out_shape=jax.ShapeDtypeStruct((rows, d), BF16),
        scratch_shapes=[pltpu.VMEM((tm, d), BF16)],
        compiler_params=_params(("parallel", "arbitrary")),
        name="gate",
    )(h2d, g, ao, yc, wga, wgb, wa, wcb)


def _outproj_body(h_ref, m_ref, w_ref, o_ref):
    o_ref[...] = h_ref[...] + jnp.dot(m_ref[...], w_ref[...], preferred_element_type=F32)


def _outproj(h2d, m, w_out):
    rows, d = h2d.shape
    tm = _row_block(rows, 640)
    tn = min(512, d)
    return pl.pallas_call(
        _outproj_body,
        grid=(rows // tm, d // tn),
        in_specs=[pl.BlockSpec((tm, tn), lambda i, j: (i, j)),
                  pl.BlockSpec((tm, d), lambda i, j: (i, 0)),
                  pl.BlockSpec((d, tn), lambda i, j: (0, j))],
        out_specs=pl.BlockSpec((tm, tn), lambda i, j: (i, j)),
        out_shape=jax.ShapeDtypeStruct((rows, d), F32),
        compiler_params=_params(("parallel", "arbitrary")),
        name="outproj",
    )(h2d, m, w_out)


def _ffn_body(h_ref, g_ref, wg_ref, wu_ref, wd_ref, o_ref, hn_ref):
    @pl.when(pl.program_id(1) == 0)
    def _():
        h = h_ref[...]
        hn_ref[...] = _rms(h, g_ref[...]).astype(BF16)
        o_ref[...] = h

    hn = hn_ref[...]
    gt = jnp.dot(hn, wg_ref[...], preferred_element_type=F32)
    up = jnp.dot(hn, wu_ref[...], preferred_element_type=F32)
    act = (gt * jax.nn.sigmoid(gt) * up).astype(BF16)
    o_ref[...] += jnp.dot(act, wd_ref[...], preferred_element_type=F32)


def _ffn(h2d, g, wg, wu, wd):
    rows, d = h2d.shape
    f = wg.shape[1]
    tm = _row_block(rows, 640)
    tf = 512
    assert f % tf == 0, f
    return pl.pallas_call(
        _ffn_body,
        grid=(rows // tm, f // tf),
        in_specs=[pl.BlockSpec((tm, d), lambda i, j: (i, 0)), pl.BlockSpec((1, d), lambda i, j: (0, 0)),
                  pl.BlockSpec((d, tf), lambda i, j: (0, j)), pl.BlockSpec((d, tf), lambda i, j: (0, j)),
                  pl.BlockSpec((tf, d), lambda i, j: (j, 0))],
        out_specs=pl.BlockSpec((tm, d), lambda i, j: (i, 0)),
        out_shape=jax.ShapeDtypeStruct((rows, d), F32),
        scratch_shapes=[pltpu.VMEM((tm, d), BF16)],
        compiler_params=_params(("parallel", "arbitrary")),
        name="ffn",
    )(h2d, g, wg, wu, wd)


def _search_threshold(count_ge, shape, top_k):
    t = jnp.where(count_ge(jnp.zeros(shape, I32)) >= top_k, 0, INT_MIN).astype(I32)

    def step(s, t):
        cand = t | (jnp.int32(1) << (30 - s))
        return jnp.where(count_ge(cand) >= top_k, cand, t)

    return lax.fori_loop(0, 31, step, t)


def _prompt_attn_body(q_ref, iq_ref, ikwq_ref, ikw_ref, k_ref, v_ref, bias_ref, o_ref,
                      iqh_ref, iwb_ref, skey_ref, selb_ref, *, top_k):
    qi = pl.program_id(1)
    ntile = qi + 1
    shape = (TILE, TILE)
    row = lax.broadcasted_iota(I32, shape, 0)
    col = lax.broadcasted_iota(I32, shape, 1)

    iq = iq_ref[0]
    iw = ikwq_ref[0][:, IDX_DIM:IDX_DIM + IDX_HEADS] * (IDX_DIM ** -0.5 * IDX_HEADS ** -0.5)
    for h in range(IDX_HEADS):
        iqh_ref[h] = iq[:, h * IDX_DIM:(h + 1) * IDX_DIM]
        iwb_ref[h] = jnp.broadcast_to(iw[:, h:h + 1], shape)

    def score_tile(kt, carry):
        off = pl.multiple_of(kt * TILE, TILE)
        ik = ikw_ref[0, pl.ds(off, TILE), :][:, :IDX_DIM].astype(BF16)
        acc = jnp.zeros(shape, F32)
        for h in range(IDX_HEADS):
            x = lax.dot_general(iqh_ref[h], ik, NT_DIMS, preferred_element_type=F32)
            acc = acc + jnp.maximum(x, 0.0) * iwb_ref[h]
        kpos = kt * TILE + col
        valid = (kpos <= qi * TILE + row) & (kpos >= FRONT_PAD)
        skey_ref[kt] = jnp.where(valid, _sortable_key(acc), INT_MIN)
        return carry

    lax.fori_loop(0, ntile, score_tile, 0)

    def count_where(pred):
        def body(kt, c):
            return c + jnp.where(pred(skey_ref[kt]), 1, 0)
        c = lax.fori_loop(0, ntile, body, jnp.zeros(shape, I32))
        return jnp.broadcast_to(jnp.sum(c, axis=-1, keepdims=True), shape)

    thr = _search_threshold(lambda cand: count_where(lambda key: key >= cand), shape, top_k)
    room = (top_k - count_where(lambda key: key > thr)).astype(F32)
    tri = (row <= col).astype(BF16)
    ones = jnp.ones(shape, BF16)

    def select_tile(kt, before):
        key = skey_ref[kt]
        eq = (key == thr) & (key != INT_MIN)
        eqb = jnp.where(eq, 1.0, 0.0).astype(BF16)
        rank = before + jnp.dot(eqb, tri, preferred_element_type=F32)
        sel = (key > thr) | (eq & (rank <= room))
        selb_ref[kt] = jnp.where(sel, 0.0, MASK_VALUE)
        return before + jnp.dot(eqb, ones, preferred_element_type=F32)

    lax.fori_loop(0, ntile, select_tile, jnp.zeros(shape, F32))

    scale = HEAD_DIM ** -0.5
    for h in range(N_HEADS):
        hs = slice(h * HEAD_DIM, (h + 1) * HEAD_DIM)
        qh = q_ref[0, :, hs]

        def attend(kt, carry, hs=hs, qh=qh, h=h):
            m, l, acc = carry
            off = pl.multiple_of(kt * TILE, TILE)
            kh = k_ref[0, pl.ds(off, TILE), hs]
            vh = v_ref[0, pl.ds(off, TILE), hs]
            s = lax.dot_general(qh, kh, NT_DIMS, preferred_element_type=F32) * scale
            s = s + bias_ref[jnp.minimum(qi - kt, 2), h] + selb_ref[kt]
            m_new = jnp.maximum(m, jnp.max(s, axis=-1, keepdims=True))
            alpha = jnp.exp(m - m_new)
            p = jnp.exp(s - m_new)
            l = alpha * l + jnp.sum(p, axis=-1, keepdims=True)
            acc = alpha * acc + jnp.dot(p.astype(BF16), vh, preferred_element_type=F32)
            return m_new, l, acc

        init = (jnp.full((TILE, 1), MASK_VALUE, F32), jnp.zeros((TILE, 1), F32), jnp.zeros(shape, F32))
        _, l, acc = lax.fori_loop(0, ntile, attend, init)
        o_ref[0, :, hs] = (acc / l).astype(BF16)


def _prompt_attn(q, iq, ikw, kb, vb, bias3, top_k):
    b, tp, _ = q.shape
    nq = tp // TILE
    qblk = lambda width: pl.BlockSpec((1, TILE, width), lambda bi, qi: (bi, qi, 0))
    seq = lambda width: pl.BlockSpec((1, tp, width), lambda bi, qi: (bi, 0, 0))
    return pl.pallas_call(
        functools.partial(_prompt_attn_body, top_k=top_k),
        grid=(b, nq),
        in_specs=[qblk(ATTN_WIDTH), qblk(IDX_WIDTH), qblk(LANES), seq(LANES), seq(ATTN_WIDTH), seq(ATTN_WIDTH),
                  pl.BlockSpec((3, N_HEADS, TILE, TILE), lambda bi, qi: (0, 0, 0, 0))],
        out_specs=qblk(ATTN_WIDTH),
        out_shape=jax.ShapeDtypeStruct((b, tp, ATTN_WIDTH), BF16),
        scratch_shapes=[pltpu.VMEM((IDX_HEADS, TILE, IDX_DIM), BF16),
                        pltpu.VMEM((IDX_HEADS, TILE, TILE), F32),
                        pltpu.VMEM((nq, TILE, TILE), I32),
                        pltpu.VMEM((nq, TILE, TILE), F32)],
        compiler_params=_params(("parallel", "arbitrary")),
        name="prompt_attn",
    )(q, iq, ikw, ikw, kb, vb, bias3)


def _sample_select_body(pt_ref, iq_ref, ikw_ref, *rest, pages_per_step, n_pages, n_new, top_k):
    page_refs = rest[:pages_per_step]
    selp_ref, seln_ref, iqm_ref, iwm_ref, skey_ref = rest[pages_per_step:]
    g = pl.program_id(1)
    shape = (n_new, TILE)
    col = lax.broadcasted_iota(I32, shape, 1)
    row = lax.broadcasted_iota(I32, shape, 0)

    @pl.when(g == 0)
    def _():
        iq = iq_ref[0].astype(F32)
        iw = ikw_ref[0][:, IDX_DIM:IDX_DIM + IDX_HEADS] * (IDX_DIM ** -0.5 * IDX_HEADS ** -0.5)
        iqm_ref[...] = jnp.concatenate(
            [iq[:, h * IDX_DIM:(h + 1) * IDX_DIM] for h in range(IDX_HEADS)], axis=0).astype(BF16)
        iwm_ref[...] = jnp.concatenate(
            [jnp.broadcast_to(iw[:, h:h + 1], shape) for h in range(IDX_HEADS)], axis=0)

    def scores(ik_bf):
        x = lax.dot_general(iqm_ref[...], ik_bf, NT_DIMS, preferred_element_type=F32)
        y = jnp.maximum(x, 0.0) * iwm_ref[...]
        return jnp.sum(y.reshape(IDX_HEADS, n_new, TILE), axis=0)

    for r in range(pages_per_step):
        s = scores(page_refs[r][0, 0].astype(BF16))
        skey_ref[g * pages_per_step + r] = _sortable_key(s)

    @pl.when(g == pl.num_programs(1) - 1)
    def _():
        ik_new = ikw_ref[0][:, :IDX_DIM]
        ik_pad = jnp.concatenate([ik_new, jnp.zeros((TILE - n_new, IDX_DIM), F32)], axis=0).astype(BF16)
        valid = (col <= row) & (col < n_new)
        skey_ref[n_pages] = jnp.where(valid, _sortable_key(scores(ik_pad)), INT_MIN)

        keys = skey_ref[...]

        def count_where(pred):
            c = jnp.sum(jnp.where(pred(keys), 1, 0), axis=0)
            return jnp.broadcast_to(jnp.sum(c, axis=-1, keepdims=True), shape)

        thr = _search_threshold(lambda cand: count_where(lambda key: key >= cand[None]), shape, top_k)
        room = (top_k - count_where(lambda key: key > thr[None])).astype(F32)
        sq = (TILE, TILE)
        tri = (lax.broadcasted_iota(I32, sq, 0) <= lax.broadcasted_iota(I32, sq, 1)).astype(BF16)
        ones = jnp.ones(sq, BF16)
        before = jnp.zeros(shape, F32)
        for p in range(n_pages + 1):
            key = keys[p]
            eq = (key == thr) & (key != INT_MIN)
            eqb = jnp.where(eq, 1.0, 0.0).astype(BF16)
            rank = before + jnp.dot(eqb, tri, preferred_element_type=F32)
            sel = (key > thr) | (eq & (rank <= room))
            selb = jnp.where(sel, 0.0, MASK_VALUE)
            if p < n_pages:
                selp_ref[0, p] = selb
            else:
                seln_ref[0] = selb
            before = before + jnp.dot(eqb, ones, preferred_element_type=F32)


def _sample_select(page_table, iq_s, ikw_s, cache_ik, top_k):
    bs, n_new, _ = iq_s.shape
    n_pages = page_table.shape[1]
    page = cache_ik.shape[2]
    assert page == TILE and cache_ik.shape[3] == IDX_DIM
    pps = min(8, n_pages)
    assert n_pages % pps == 0
    page_spec = lambda r: pl.BlockSpec(
        (1, 1, page, IDX_DIM), lambda b, g, pt, r=r: (0, pt[b, g * pps + r], 0, 0))
    grid_spec = pltpu.PrefetchScalarGridSpec(
        num_scalar_prefetch=1,
        grid=(bs, n_pages // pps),
        in_specs=[pl.BlockSpec((1, n_new, IDX_WIDTH), lambda b, g, pt: (b, 0, 0)),
                  pl.BlockSpec((1, n_new, LANES), lambda b, g, pt: (b, 0, 0))]
                 + [page_spec(r) for r in range(pps)],
        out_specs=[pl.BlockSpec((1, n_pages, n_new, TILE), lambda b, g, pt: (b, 0, 0, 0)),
                   pl.BlockSpec((1, n_new, TILE), lambda b, g, pt: (b, 0, 0))],
        scratch_shapes=[pltpu.VMEM((IDX_HEADS * n_new, IDX_DIM), BF16),
                        pltpu.VMEM((IDX_HEADS * n_new, TILE), F32),
                        pltpu.VMEM((n_pages + 1, n_new, TILE), I32)],
    )
    return pl.pallas_call(
        functools.partial(_sample_select_body, pages_per_step=pps, n_pages=n_pages, n_new=n_new, top_k=top_k),
        grid_spec=grid_spec,
        out_shape=(jax.ShapeDtypeStruct((bs, n_pages, n_new, TILE), F32),
                   jax.ShapeDtypeStruct((bs, n_new, TILE), F32)),
        compiler_params=_params(("parallel", "arbitrary")),
        name="sample_select",
    )(page_table, iq_s, ikw_s, *([cache_ik] * pps))


def _sample_attn_body(pt_ref, q_ref, selp_ref, seln_ref, knew_ref, vnew_ref, expand_ref, hmask_ref,
                      bias_ref, bnew_ref, *rest, pages_per_step, n_pages, n_new):
    k_refs = rest[:pages_per_step]
    v_refs = rest[pages_per_step:2 * pages_per_step]
    o_ref, qm_ref, m_ref, l_ref, acc_ref = rest[2 * pages_per_step:]
    g = pl.program_id(1)
    nrow = N_HEADS * n_new
    scale = HEAD_DIM ** -0.5

    @pl.when(g == 0)
    def _():
        q = q_ref[0].astype(F32)
        qm_ref[...] = jnp.concatenate(
            [q[:, h * HEAD_DIM:(h + 1) * HEAD_DIM] for h in range(N_HEADS)], axis=0).astype(BF16)
        m_ref[...] = jnp.full((nrow, LANES), MASK_VALUE, F32)
        l_ref[...] = jnp.zeros((nrow, LANES), F32)
        acc_ref[...] = jnp.zeros((nrow, HEAD_DIM), F32)

    def update(s, v_bf):
        m = m_ref[...]
        m_new = jnp.maximum(m, jnp.max(s, axis=-1, keepdims=True))
        alpha = jnp.exp(m - m_new)
        p = jnp.exp(s - m_new[:, :1])
        l_ref[...] = alpha * l_ref[...] + jnp.sum(p, axis=-1, keepdims=True)
        acc_ref[...] = alpha * acc_ref[...] + jnp.dot(p.astype(BF16), v_bf, preferred_element_type=F32)
        m_ref[...] = m_new

    def expand_sel(selb):
        x = jnp.dot(selb.astype(BF16), expand_ref[...], preferred_element_type=F32)
        return jnp.concatenate([x] * N_HEADS, axis=0)

    for r in range(pages_per_step):
        p_idx = g * pages_per_step + r
        kp = k_refs[r][0, 0].reshape(TILE * N_HEADS, HEAD_DIM).astype(BF16)
        vp = v_refs[r][0, 0].reshape(TILE * N_HEADS, HEAD_DIM).astype(BF16)
        s = lax.dot_general(qm_ref[...], kp, NT_DIMS, preferred_element_type=F32) * scale
        near = (p_idx == n_pages - 1).astype(I32)
        s = s + bias_ref[near] + hmask_ref[...] + expand_sel(selp_ref[0, r])
        update(s, vp)

    @pl.when(g == pl.num_programs(1) - 1)
    def _():
        pad = jnp.zeros((TILE - n_new * N_HEADS, HEAD_DIM), F32)
        kn = jnp.concatenate([knew_ref[0].reshape(n_new * N_HEADS, HEAD_DIM), pad], axis=0).astype(BF16)
        vn = jnp.concatenate([vnew_ref[0].reshape(n_new * N_HEADS, HEAD_DIM), pad], axis=0).astype(BF16)
        s = lax.dot_general(qm_ref[...], kn, NT_DIMS, preferred_element_type=F32) * scale
        s = s + bnew_ref[...] + expand_sel(seln_ref[0])[:, :TILE]
        update(s, vn)
        o_ref[0] = acc_ref[...] / l_ref[...]


def _sample_attn(page_table, q_s, selp, seln, k_new, v_new, tables, cache_k, cache_v):
    bs, n_new, _ = q_s.shape
    n_pages = page_table.shape[1]
    page = cache_k.shape[2]
    assert page == TILE and cache_k.shape[3:] == (N_HEADS, HEAD_DIM)
    assert n_new * N_HEADS <= TILE
    pps = min(4, n_pages)
    assert n_pages % pps == 0
    nrow = N_HEADS * n_new
    wide = TILE * N_HEADS
    kv_spec = lambda r: pl.BlockSpec(
        (1, 1, page, N_HEADS, HEAD_DIM), lambda b, g, pt, r=r: (0, pt[b, g * pps + r], 0, 0, 0))
    const = lambda shape: pl.BlockSpec(shape, lambda b, g, pt: (0,) * len(shape))
    per_seq = lambda shape: pl.BlockSpec((1,) + shape, lambda b, g, pt: (b,) + (0,) * len(shape))
    grid_spec = pltpu.PrefetchScalarGridSpec(
        num_scalar_prefetch=1,
        grid=(bs, n_pages // pps),
        in_specs=[per_seq((n_new, ATTN_WIDTH)),
                  pl.BlockSpec((1, pps, n_new, TILE), lambda b, g, pt: (b, g, 0, 0)),
                  per_seq((n_new, TILE)),
                  per_seq((n_new, N_HEADS, HEAD_DIM)), per_seq((n_new, N_HEADS, HEAD_DIM)),
                  const((TILE, wide)), const((nrow, wide)), const((2, nrow, wide)), const((nrow, TILE))]
                 + [kv_spec(r) for r in range(pps)] + [kv_spec(r) for r in range(pps)],
        out_specs=per_seq((nrow, HEAD_DIM)),
        scratch_shapes=[pltpu.VMEM((nrow, HEAD_DIM), BF16), pltpu.VMEM((nrow, LANES), F32),
                        pltpu.VMEM((nrow, LANES), F32), pltpu.VMEM((nrow, HEAD_DIM), F32)],
    )
    return pl.pallas_call(
        functools.partial(_sample_attn_body, pages_per_step=pps, n_pages=n_pages, n_new=n_new),
        grid_spec=grid_spec,
        out_shape=jax.ShapeDtypeStruct((bs, nrow, HEAD_DIM), F32),
        compiler_params=_params(("parallel", "arbitrary")),
        name="sample_attn",
    )(page_table, q_s, selp, seln, k_new, v_new, tables["expand"], tables["hmask"], tables["bias"],
      tables["bnew"], *([cache_k] * pps), *([cache_v] * pps))


def _t5_bucket(rel):
    n = jnp.maximum(rel, 0)
    max_exact = REL_BUCKETS // 2
    nf = jnp.maximum(n, 1).astype(F32)
    large = max_exact + (jnp.log(nf / max_exact) / math.log(REL_MAX_DIST / max_exact)
                         * (REL_BUCKETS - max_exact)).astype(I32)
    large = jnp.minimum(large, REL_BUCKETS - 1)
    return jnp.where(n < max_exact, n, large)


def _far_bucket_is_constant():
    n = np.arange(TILE + 1, 1 << 20, dtype=np.float64)
    large = REL_BUCKETS // 2 + np.floor(np.log(n / (REL_BUCKETS // 2)) / math.log(REL_MAX_DIST / (REL_BUCKETS // 2))
                                        * (REL_BUCKETS - REL_BUCKETS // 2) - 1e-6)
    return bool(np.all(large >= REL_BUCKETS - 1))


def _prompt_bias_tables(rel_bias):
    assert _far_bucket_is_constant()
    i = np.arange(TILE)[:, None]
    j = np.arange(TILE)[None, :]
    rel = jnp.asarray(np.stack([i - j, TILE + i - j, np.full((TILE, TILE), 2 * TILE)]), I32)
    return jnp.moveaxis(rel_bias.astype(F32)[_t5_bucket(rel)], -1, 1)


def _sample_tables(rel_bias, n_new, past_len):
    assert _far_bucket_is_constant()
    nrow = N_HEADS * n_new
    wide = TILE * N_HEADS
    rh = np.arange(nrow)[:, None] // n_new
    rq = np.arange(nrow)[:, None] % n_new
    cs = np.arange(wide)[None, :] // N_HEADS
    ch = np.arange(wide)[None, :] % N_HEADS
    rb = rel_bias.astype(F32)
    head_match = jnp.asarray(rh == ch)
    rel_near = jnp.asarray(TILE + rq - cs, I32)
    rel_far = jnp.full((nrow, wide), 2 * TILE, I32)
    rh_b = jnp.asarray(np.broadcast_to(rh, (nrow, wide)))
    bias = jnp.stack([rb[_t5_bucket(rel_far), rh_b], rb[_t5_bucket(rel_near), rh_b]])
    hmask = jnp.where(head_match, 0.0, MASK_VALUE).astype(F32)
    cj = np.arange(TILE)[None, :] // N_HEADS
    chn = np.arange(TILE)[None, :] % N_HEADS
    ok = (rh == chn) & (cj < n_new)
    rel_new = jnp.asarray(np.clip(rq - cj, 0, None), I32)
    rh_n = jnp.asarray(np.broadcast_to(rh, (nrow, TILE)))
    bnew = jnp.where(jnp.asarray(ok), rb[_t5_bucket(rel_new), rh_n], MASK_VALUE)
    expand = np.zeros((TILE, wide), np.float32)
    expand[cs[0], np.arange(wide)] = 1.0
    return {"expand": jnp.asarray(expand, BF16), "hmask": hmask, "bias": bias, "bnew": bnew}


def _split_w_in(w_in, d, c):
    sizes = (ATTN_WIDTH, ATTN_WIDTH, ATTN_WIDTH, IDX_WIDTH, IDX_DIM, IDX_HEADS, c, c, d, d)
    offs = np.concatenate([[0], np.cumsum(sizes)])
    assert offs[-1] == w_in.shape[1], (offs[-1], w_in.shape)
    names = ("q", "k", "v", "iq", "ik", "iw", "ca", "cb", "ga", "gb")
    parts = {n: w_in[:, offs[i]:offs[i + 1]].astype(BF16) for i, n in enumerate(names)}
    pad = jnp.zeros((d, LANES - IDX_DIM - IDX_HEADS), BF16)
    parts["ikw"] = jnp.concatenate([parts.pop("ik"), parts.pop("iw"), pad], axis=1)
    return parts


def kernel(x_prompt, x_sample, cache_k, cache_v, cache_ik, state_conv, page_table, meta_tokens, rel_bias,
           norm_mix_g, w_in, q_norm_g, k_norm_g, w_attn_br, w_dw, b_dw, ln_conv_g, ln_conv_b, w_conv_br,
           w_out, norm_ffn_g, w_ffn_gate, w_ffn_up, w_ffn_down):
    assert w_in.shape[0] == 1, "single trunk layer"
    b, seq, d = x_prompt.shape
    bs, n_new, _ = x_sample.shape
    c = w_dw.shape[2]
    n_pages = page_table.shape[1]
    past_len = n_pages * cache_k.shape[2]
    assert seq % TILE == 0 and FRONT_PAD >= CONV_CTX
    tp = seq + TILE
    t_real = seq + N_META

    w = _split_w_in(w_in[0], d, c)
    g_mix = norm_mix_g[0][None]
    g_ffn = norm_ffn_g[0][None]
    qg = q_norm_g[0][None]
    kg = k_norm_g[0][None]
    wa = w_attn_br[0].astype(BF16)
    wcb = w_conv_br[0].astype(BF16)
    wo = w_out[0].astype(BF16)
    wg = w_ffn_gate[0].astype(BF16)
    wu = w_ffn_up[0].astype(BF16)
    wd = w_ffn_down[0].astype(BF16)
    conv_w = (w_dw[0], b_dw[0][None], ln_conv_g[0][None], ln_conv_b[0][None])

    def tail(h2d, ao2d, yc2d):
        m = _gate(h2d, g_mix, ao2d, yc2d, w["ga"], w["gb"], wa, wcb)
        return _ffn(_outproj(h2d, m, wo), g_ffn, wg, wu, wd)

    hp = jnp.concatenate([jnp.zeros((b, FRONT_PAD, d), x_prompt.dtype),
                          jnp.broadcast_to(meta_tokens[None].astype(x_prompt.dtype), (b, N_META, d)),
                          x_prompt], axis=1).reshape(b * tp, d)
    q, kf, kb, vf, vb, iq, u, ikw = _inproj(hp, g_mix, w, qg, kg)
    r3 = lambda a: a.reshape(b, tp, a.shape[-1])
    ao = _prompt_attn(r3(q), r3(iq), r3(ikw), r3(kb), r3(vb), _prompt_bias_tables(rel_bias),
                      min(TOPK_MAX, seq // 4))
    yc = _conv(r3(u), *conv_w)
    y_p = tail(hp, ao.reshape(b * tp, ATTN_WIDTH), yc.reshape(b * tp, c))
    y_prompt = y_p.reshape(b, tp, d)[:, TILE:]
    k_prompt = r3(kf)[:, FRONT_PAD:].reshape(1, b, t_real, N_HEADS, HEAD_DIM)
    v_prompt = r3(vf)[:, FRONT_PAD:].reshape(1, b, t_real, N_HEADS, HEAD_DIM)
    ik_prompt = r3(ikw)[None, :, FRONT_PAD:, :IDX_DIM]
    conv_prompt = r3(u)[None, :, tp - CONV_CTX:]

    hs = x_sample.reshape(bs * n_new, d)
    q, kf, kb, vf, vb, iq, u, ikw = _inproj(hs, g_mix, w, qg, kg)
    s3 = lambda a: a.reshape(bs, n_new, a.shape[-1])
    s4 = lambda a: a.reshape(bs, n_new, N_HEADS, HEAD_DIM)
    top_k = min(TOPK_MAX, (past_len + n_new) // 4)
    selp, seln = _sample_select(page_table, s3(iq), s3(ikw), cache_ik, top_k)
    ao_s = _sample_attn(page_table, s3(q), selp, seln, s4(kf), s4(vf),
                        _sample_tables(rel_bias, n_new, past_len), cache_k, cache_v)
    ao_s = ao_s.reshape(bs, N_HEADS, n_new, HEAD_DIM).transpose(0, 2, 1, 3).reshape(bs * n_new, ATTN_WIDTH)
    full = jnp.concatenate([state_conv[0].astype(F32), s3(u)], axis=1)
    t_full = CONV_CTX + n_new
    t_pad = -(-t_full // SUBLANES) * SUBLANES
    full_pad = jnp.concatenate([full, jnp.zeros((bs, t_pad - t_full, c), F32)], axis=1)
    yc_s = _conv(full_pad, *conv_w)[:, CONV_CTX:t_full].reshape(bs * n_new, c)
    y_sample = tail(hs, ao_s.astype(BF16), yc_s).reshape(bs, n_new, d)
    k_sample = s4(kf)[None]
    v_sample = s4(vf)[None]
    ik_sample = s3(ikw)[None, :, :, :IDX_DIM]
    conv_sample = full[None, :, n_new:]

    return (y_prompt, y_sample, k_prompt, v_prompt, ik_prompt, conv_prompt,
            k_sample, v_sample, ik_sample, conv_sample)
```

```python
import functools
import math

import numpy as np
import jax
import jax.numpy as jnp
from jax import lax
from jax.experimental import pallas as pl
from jax.experimental.pallas import tpu as pltpu

N_META = 16
N_HEADS = 8
HEAD_DIM = 128
ATTN_WIDTH = N_HEADS * HEAD_DIM
IDX_HEADS = 16
IDX_DIM = 64
IDX_WIDTH = IDX_HEADS * IDX_DIM
TOPK_MAX = 256
REL_BUCKETS = 32
REL_MAX_DIST = 128
CONV_WIDTH = 31
CONV_CTX = CONV_WIDTH - 1
NORM_EPS = 1e-6
MASK_VALUE = -1e30

LANES = 128
SUBLANES = 8
TILE = LANES
QB = 2 * TILE
HALO = 32
INT_MIN = -2 ** 31
VMEM_LIMIT = 52 * 1024 * 1024
ROW_BLOCK_CAP = 512

F32 = jnp.float32
BF16 = jnp.bfloat16
I32 = jnp.int32
NT_DIMS = (((1,), (1,)), ((), ()))
IDX_SCALE = IDX_DIM ** -0.5 * IDX_HEADS ** -0.5
ATTN_SCALE = HEAD_DIM ** -0.5


def _row_block(rows, cap=ROW_BLOCK_CAP):
    best = None
    for d in range(SUBLANES, min(rows, cap) + 1, SUBLANES):
        if rows % d == 0:
            best = d
    assert best is not None, rows
    return best


def _params(sem):
    return pltpu.CompilerParams(dimension_semantics=sem, vmem_limit_bytes=VMEM_LIMIT)


def _rms(x, g):
    ms = jnp.mean(x * x, axis=-1, keepdims=True)
    return x * lax.rsqrt(ms + NORM_EPS) * g


def _sortable_key(s):
    s = jnp.where(s == 0.0, 0.0, s)
    bits = lax.bitcast_convert_type(s, I32)
    return jnp.where(bits < 0, bits ^ jnp.int32(0x7FFFFFFF), bits)


def _inproj_body(h_ref, g_ref, wq_ref, wk_ref, wv_ref, wiq_ref, wa_ref, wb_ref, wikw_ref, qg_ref, kg_ref,
                 q_ref, kf_ref, kb_ref, vf_ref, vb_ref, iq_ref, u_ref, ikw_ref, xn_ref, *, tn):
    j = pl.program_id(1)

    @pl.when(j == 0)
    def _():
        xn_ref[...] = _rms(h_ref[...], g_ref[...]).astype(BF16)
        ikw_ref[...] = jnp.dot(xn_ref[...], wikw_ref[...], preferred_element_type=F32)

    xn = xn_ref[...]

    def mm(w_ref):
        return jnp.dot(xn, w_ref[...], preferred_element_type=F32)

    def head_norm(z, g):
        cols = [_rms(z[:, c * HEAD_DIM:(c + 1) * HEAD_DIM], g) for c in range(tn // HEAD_DIM)]
        return jnp.concatenate(cols, axis=-1)

    q_ref[...] = head_norm(mm(wq_ref), qg_ref[...]).astype(BF16)
    k = head_norm(mm(wk_ref), kg_ref[...])
    kf_ref[...] = k
    kb_ref[...] = k.astype(BF16)
    v = mm(wv_ref)
    vf_ref[...] = v
    vb_ref[...] = v.astype(BF16)
    iq_ref[...] = mm(wiq_ref).astype(BF16)
    u_ref[...] = mm(wa_ref) * jax.nn.sigmoid(mm(wb_ref))


def _inproj(h2d, g, w, qg, kg):
    rows, d = h2d.shape
    c = w["ca"].shape[1]
    assert c == ATTN_WIDTH, "fused input projection assumes CONV_CH == ATTN_WIDTH"
    tm = _row_block(rows)
    tn = 256
    nj = ATTN_WIDTH // tn
    row_spec = lambda width: pl.BlockSpec((tm, width), lambda i, j: (i, j))
    w_spec = pl.BlockSpec((d, tn), lambda i, j: (0, j))
    const = lambda shape: pl.BlockSpec(shape, lambda i, j: (0, 0))
    out_shape = (
        jax.ShapeDtypeStruct((rows, ATTN_WIDTH), BF16),
        jax.ShapeDtypeStruct((rows, ATTN_WIDTH), F32),
        jax.ShapeDtypeStruct((rows, ATTN_WIDTH), BF16),
        jax.ShapeDtypeStruct((rows, ATTN_WIDTH), F32),
        jax.ShapeDtypeStruct((rows, ATTN_WIDTH), BF16),
        jax.ShapeDtypeStruct((rows, IDX_WIDTH), BF16),
        jax.ShapeDtypeStruct((rows, c), F32),
        jax.ShapeDtypeStruct((rows, LANES), F32),
    )
    return pl.pallas_call(
        functools.partial(_inproj_body, tn=tn),
        grid=(rows // tm, nj),
        in_specs=[pl.BlockSpec((tm, d), lambda i, j: (i, 0)), const((1, d)),
                  w_spec, w_spec, w_spec, w_spec, w_spec, w_spec, const((d, LANES)),
                  const((1, HEAD_DIM)), const((1, HEAD_DIM))],
        out_specs=[row_spec(tn)] * 7 + [pl.BlockSpec((tm, LANES), lambda i, j: (i, 0))],
        out_shape=out_shape,
        scratch_shapes=[pltpu.VMEM((tm, d), BF16)],
        compiler_params=_params(("parallel", "arbitrary")),
        name="inproj",
    )(h2d, g, w["q"], w["k"], w["v"], w["iq"], w["ca"], w["cb"], w["ikw"], qg, kg)


def _conv_body(x_ref, halo_ref, w_ref, b_ref, lg_ref, lb_ref, o_ref, buf_ref, y_ref, *, tt, c):
    @pl.when(pl.program_id(1) == 0)
    def _():
        buf_ref[0:HALO, :] = halo_ref[0]

    buf_ref[HALO:HALO + tt, :] = x_ref[0]
    lead = HALO - CONV_CTX
    rsum = jnp.zeros((tt, LANES), F32)
    for cc in range(c // LANES):
        cs = slice(cc * LANES, (cc + 1) * LANES)
        acc = jnp.zeros((tt, LANES), F32) + b_ref[:, cs]
        for j in range(CONV_WIDTH):
            acc = acc + w_ref[j:j + 1, cs] * buf_ref[lead + j:lead + j + tt, cs]
        y_ref[:, cs] = acc
        rsum = rsum + acc
    mu = jnp.sum(rsum, axis=-1, keepdims=True) * (1.0 / c)
    vsum = jnp.zeros((tt, LANES), F32)
    for cc in range(c // LANES):
        cs = slice(cc * LANES, (cc + 1) * LANES)
        dlt = y_ref[:, cs] - mu
        vsum = vsum + dlt * dlt
    inv = lax.rsqrt(jnp.sum(vsum, axis=-1, keepdims=True) * (1.0 / c) + NORM_EPS)
    for cc in range(c // LANES):
        cs = slice(cc * LANES, (cc + 1) * LANES)
        z = (y_ref[:, cs] - mu) * inv * lg_ref[:, cs] + lb_ref[:, cs]
        o_ref[0, :, cs] = (z * jax.nn.sigmoid(z)).astype(BF16)
    buf_ref[0:HALO, :] = buf_ref[tt:tt + HALO, :]


def _conv(x, halo, w_dw, b_dw, ln_g, ln_b):
    n, t, c = x.shape
    tt = _row_block(t, 128)
    halo_map = (lambda b, i: (b, 0, 0)) if halo.shape[0] == n else (lambda b, i: (0, 0, 0))
    const = lambda shape: pl.BlockSpec(shape, lambda b, i: (0, 0))
    blk = pl.BlockSpec((1, tt, c), lambda b, i: (b, i, 0))
    return pl.pallas_call(
        functools.partial(_conv_body, tt=tt, c=c),
        grid=(n, t // tt),
        in_specs=[blk, pl.BlockSpec((1, HALO, c), halo_map),
                  const((CONV_WIDTH, c)), const((1, c)), const((1, c)), const((1, c))],
        out_specs=blk,
        out_shape=jax.ShapeDtypeStruct((n, t, c), BF16),
        scratch_shapes=[pltpu.VMEM((HALO + tt, c), F32), pltpu.VMEM((tt, c), F32)],
        compiler_params=_params(("parallel", "arbitrary")),
        name="conv",
    )(x, halo, w_dw, b_dw, ln_g, ln_b)


def _gate_body(h_ref, g_ref, ao_ref, yc_ref, wga_ref, wgb_ref, wa_ref, wcb_ref, m_ref, xn_ref):
    @pl.when(pl.program_id(1) == 0)
    def _():
        xn_ref[...] = _rms(h_ref[...], g_ref[...]).astype(BF16)

    xn = xn_ref[...]
    ga = jnp.dot(xn, wga_ref[...], preferred_element_type=F32)
    gb = jnp.dot(xn, wgb_ref[...], preferred_element_type=F32)
    a = jnp.dot(ao_ref[...], wa_ref[...], preferred_element_type=F32)
    co = jnp.dot(yc_ref[...], wcb_ref[...], preferred_element_type=F32)
    m_ref[...] = (jax.nn.sigmoid(ga) * a + jax.nn.sigmoid(gb) * co).astype(BF16)


def _gate(h2d, g, ao, yc, wga, wgb, wa, wcb):
    rows, d = h2d.shape
    tm = _row_block(rows)
    tn = 256
    full_rows = lambda width: pl.BlockSpec((tm, width), lambda i, j: (i, 0))
    wcol = lambda k: pl.BlockSpec((k, tn), lambda i, j: (0, j))
    return pl.pallas_call(
        _gate_body,
        grid=(rows // tm, d // tn),
        in_specs=[full_rows(d), pl.BlockSpec((1, d), lambda i, j: (0, 0)),
                  full_rows(ao.shape[1]), full_rows(yc.shape[1]),
                  wcol(d), wcol(d), wcol(wa.shape[0]), wcol(wcb.shape[0])],
        out_specs=pl.BlockSpec((tm, tn), lambda i, j: (i, j)),
        out_shape=jax.ShapeDtypeStruct((rows, d), BF16),
        scratch_shapes=[pltpu.VMEM((tm, d), BF16)],
        compiler_params=_params(("parallel", "arbitrary")),
        name="gate",
    )(h2d, g, ao, yc, wga, wgb, wa, wcb)


def _outproj_body(h_ref, m_ref, w_ref, o_ref):
    o_ref[...] = h_ref[...] + jnp.dot(m_ref[...], w_ref[...], preferred_element_type=F32)


def _outproj(h2d, m, w_out):
    rows, d = h2d.shape
    tm = _row_block(rows)
    tn = min(512, d)
    return pl.pallas_call(
        _outproj_body,
        grid=(rows // tm, d // tn),
        in_specs=[pl.BlockSpec((tm, tn), lambda i, j: (i, j)),
                  pl.BlockSpec((tm, d), lambda i, j: (i, 0)),
                  pl.BlockSpec((d, tn), lambda i, j: (0, j))],
        out_specs=pl.BlockSpec((tm, tn), lambda i, j: (i, j)),
        out_shape=jax.ShapeDtypeStruct((rows, d), F32),
        compiler_params=_params(("parallel", "arbitrary")),
        name="outproj",
    )(h2d, m, w_out)


def _ffn_body(h_ref, g_ref, wg_ref, wu_ref, wd_ref, o_ref, hn_ref):
    @pl.when(pl.program_id(1) == 0)
    def _():
        h = h_ref[...]
        hn_ref[...] = _rms(h, g_ref[...]).astype(BF16)
        o_ref[...] = h

    hn = hn_ref[...]
    gt = jnp.dot(hn, wg_ref[...], preferred_element_type=F32)
    up = jnp.dot(hn, wu_ref[...], preferred_element_type=F32)
    act = (gt * jax.nn.sigmoid(gt) * up).astype(BF16)
    o_ref[...] += jnp.dot(act, wd_ref[...], preferred_element_type=F32)


def _ffn(h2d, g, wg, wu, wd):
    rows, d = h2d.shape
    f = wg.shape[1]
    tm = _row_block(rows)
    tf = 512
    assert f % tf == 0, f
    return pl.pallas_call(
        _ffn_body,
        grid=(rows // tm, f // tf),
        in_specs=[pl.BlockSpec((tm, d), lambda i, j: (i, 0)), pl.BlockSpec((1, d), lambda i, j: (0, 0)),
                  pl.BlockSpec((d, tf), lambda i, j: (0, j)), pl.BlockSpec((d, tf), lambda i, j: (0, j)),
                  pl.BlockSpec((tf, d), lambda i, j: (j, 0))],
        out_specs=pl.BlockSpec((tm, d), lambda i, j: (i, 0)),
        out_shape=jax.ShapeDtypeStruct((rows, d), F32),
        scratch_shapes=[pltpu.VMEM((tm, d), BF16)],
        compiler_params=_params(("parallel", "arbitrary")),
        name="ffn",
    )(h2d, g, wg, wu, wd)


def _select_top_k(skey_ref, selb_ref, ntile, shape, top_k):
    sq = (TILE, TILE)

    def count_where(pred):
        def body(t, c):
            return c + jnp.where(pred(skey_ref[t]), 1, 0)
        c = lax.fori_loop(0, ntile, body, jnp.zeros(shape, I32))
        return jnp.broadcast_to(jnp.sum(c, axis=-1, keepdims=True), shape)

    n_pos = count_where(lambda key: key >= 0)
    thr0 = jnp.where(n_pos >= top_k, 0, INT_MIN).astype(I32)

    def step(s, carry):
        thr, n_thr = carry
        cand = thr | (jnp.int32(1) << (30 - s))
        n = count_where(lambda key: key >= cand)
        ok = n >= top_k
        return jnp.where(ok, cand, thr), jnp.where(ok, n, n_thr)

    thr, n_thr = lax.fori_loop(0, 31, step, (thr0, jnp.where(n_pos >= top_k, n_pos, 0)))
    ties_matter = jnp.max(n_thr.astype(F32)) > top_k

    @pl.when(jnp.logical_not(ties_matter))
    def _():
        def body(t, carry):
            key = skey_ref[t]
            selb_ref[t] = jnp.where((key >= thr) & (key != INT_MIN), 0.0, MASK_VALUE)
            return carry
        lax.fori_loop(0, ntile, body, 0)

    @pl.when(ties_matter)
    def _():
        room = (top_k - count_where(lambda key: key > thr)).astype(F32)
        tri = (lax.broadcasted_iota(I32, sq, 0) <= lax.broadcasted_iota(I32, sq, 1)).astype(BF16)
        ones = jnp.ones(sq, BF16)

        def body(t, before):
            key = skey_ref[t]
            eq = (key == thr) & (key != INT_MIN)
            eqb = jnp.where(eq, 1.0, 0.0).astype(BF16)
            rank = before + jnp.dot(eqb, tri, preferred_element_type=F32)
            sel = (key > thr) | (eq & (rank <= room))
            selb_ref[t] = jnp.where(sel, 0.0, MASK_VALUE)
            return before + jnp.dot(eqb, ones, preferred_element_type=F32)

        lax.fori_loop(0, ntile, body, jnp.zeros(shape, F32))


def _prompt_attn_body(q_ref, iq_ref, ikwq_ref, ikw_ref, k_ref, v_ref, kmeta_ref, vmeta_ref, ikwm_ref,
                      bias_ref, bmeta_ref, o_ref,
                      iqh_ref, iwb_ref, skey_ref, selb_ref, m_ref, l_ref, acc_ref, *, top_k):
    j = pl.program_id(1)
    nchunk = j + 1
    shape = (QB, TILE)
    row = lax.broadcasted_iota(I32, (QB, QB), 0)
    col = lax.broadcasted_iota(I32, (QB, QB), 1)

    iq = iq_ref[0]
    iw = ikwq_ref[0][:, IDX_DIM:IDX_DIM + IDX_HEADS] * IDX_SCALE
    for h in range(IDX_HEADS):
        iqh_ref[h] = iq[:, h * IDX_DIM:(h + 1) * IDX_DIM]
        iwb_ref[h] = jnp.broadcast_to(iw[:, h:h + 1], shape)

    def index_scores(ik_bf, width):
        acc = jnp.zeros((QB, width), F32)
        for h in range(IDX_HEADS):
            x = lax.dot_general(iqh_ref[h], ik_bf, NT_DIMS, preferred_element_type=F32)
            wgt = iwb_ref[h]
            acc = acc + jnp.maximum(x, 0.0) * jnp.concatenate([wgt] * (width // TILE), axis=1)
        return acc

    s_meta = index_scores(ikwm_ref[:, :IDX_DIM].astype(BF16), TILE)
    is_meta = lax.broadcasted_iota(I32, shape, 1) < N_META
    skey_ref[0] = jnp.where(is_meta, _sortable_key(s_meta), INT_MIN)

    def score_chunk(c, carry):
        off = pl.multiple_of(c * QB, QB)
        ik = ikw_ref[0, pl.ds(off, QB), :][:, :IDX_DIM].astype(BF16)
        key = jnp.where(c * QB + col <= j * QB + row, _sortable_key(index_scores(ik, QB)), INT_MIN)
        skey_ref[1 + 2 * c] = key[:, :TILE]
        skey_ref[2 + 2 * c] = key[:, TILE:]
        return carry

    lax.fori_loop(0, nchunk, score_chunk, 0)

    _select_top_k(skey_ref, selb_ref, 1 + 2 * nchunk, shape, top_k)

    far_meta = jnp.minimum(j, 1)
    for h in range(N_HEADS):
        hs = slice(h * HEAD_DIM, (h + 1) * HEAD_DIM)
        s = lax.dot_general(q_ref[0, :, hs], kmeta_ref[:, hs], NT_DIMS, preferred_element_type=F32) * ATTN_SCALE
        s = s + bmeta_ref[far_meta, h] + selb_ref[0]
        m = jnp.max(s, axis=-1, keepdims=True)
        p = jnp.exp(s - m)
        m_ref[h] = jnp.broadcast_to(m, shape)
        l_ref[h] = jnp.broadcast_to(jnp.sum(p, axis=-1, keepdims=True), shape)
        acc_ref[:, hs] = jnp.dot(p.astype(BF16), vmeta_ref[:, hs], preferred_element_type=F32)

    def attend_chunk(c, carry):
        off = pl.multiple_of(c * QB, QB)
        selb = jnp.concatenate([selb_ref[1 + 2 * c], selb_ref[2 + 2 * c]], axis=1)
        dist = 2 * (j - c)
        idx = [[jnp.clip(dist + r - e, 0, 2) for e in range(2)] for r in range(2)]
        for h in range(N_HEADS):
            hs = slice(h * HEAD_DIM, (h + 1) * HEAD_DIM)
            kh = k_ref[0, pl.ds(off, QB), hs]
            vh = v_ref[0, pl.ds(off, QB), hs]
            s = lax.dot_general(q_ref[0, :, hs], kh, NT_DIMS, preferred_element_type=F32) * ATTN_SCALE
            bias = jnp.concatenate(
                [jnp.concatenate([bias_ref[idx[r][0], h], bias_ref[idx[r][1], h]], axis=1) for r in range(2)],
                axis=0)
            s = s + bias + selb
            m_old = m_ref[h]
            m_new = jnp.maximum(m_old, jnp.max(s, axis=-1, keepdims=True))
            alpha = jnp.exp(m_old - m_new)
            p = jnp.exp(s - jnp.concatenate([m_new, m_new], axis=1))
            l_ref[h] = alpha * l_ref[h] + jnp.sum(p, axis=-1, keepdims=True)
            acc_ref[:, hs] = alpha * acc_ref[:, hs] + jnp.dot(p.astype(BF16), vh, preferred_element_type=F32)
            m_ref[h] = m_new
        return carry

    lax.fori_loop(0, nchunk, attend_chunk, 0)

    for h in range(N_HEADS):
        hs = slice(h * HEAD_DIM, (h + 1) * HEAD_DIM)
        o_ref[0, :, hs] = (acc_ref[:, hs] / l_ref[h]).astype(BF16)


def _prompt_attn(q, iq, ikw, kb, vb, kmeta, vmeta, ikwm, bias3, bmeta, top_k):
    b, t, _ = q.shape
    nq = t // QB
    ntile = 1 + 2 * nq
    qblk = lambda width: pl.BlockSpec((1, QB, width), lambda bi, qi: (bi, qi, 0))
    seq = lambda width: pl.BlockSpec((1, t, width), lambda bi, qi: (bi, 0, 0))
    const = lambda shape: pl.BlockSpec(shape, lambda bi, qi: (0,) * len(shape))
    return pl.pallas_call(
        functools.partial(_prompt_attn_body, top_k=top_k),
        grid=(b, nq),
        in_specs=[qblk(ATTN_WIDTH), qblk(IDX_WIDTH), qblk(LANES), seq(LANES), seq(ATTN_WIDTH), seq(ATTN_WIDTH),
                  const((TILE, ATTN_WIDTH)), const((TILE, ATTN_WIDTH)), const((TILE, LANES)),
                  const((3, N_HEADS, TILE, TILE)), const((2, N_HEADS, QB, TILE))],
        out_specs=qblk(ATTN_WIDTH),
        out_shape=jax.ShapeDtypeStruct((b, t, ATTN_WIDTH), BF16),
        scratch_shapes=[pltpu.VMEM((IDX_HEADS, QB, IDX_DIM), BF16),
                        pltpu.VMEM((IDX_HEADS, QB, TILE), F32),
                        pltpu.VMEM((ntile, QB, TILE), I32),
                        pltpu.VMEM((ntile, QB, TILE), F32),
                        pltpu.VMEM((N_HEADS, QB, TILE), F32),
                        pltpu.VMEM((N_HEADS, QB, TILE), F32),
                        pltpu.VMEM((QB, ATTN_WIDTH), F32)],
        compiler_params=_params(("parallel", "arbitrary")),
        name="prompt_attn",
    )(q, iq, ikw, ikw, kb, vb, kmeta, vmeta, ikwm, bias3, bmeta)


def _sample_scores_body(pt_ref, iq_ref, ikw_ref, *rest, pages_per_step, n_pages, n_new):
    page_refs = rest[:pages_per_step]
    skey_ref, iqm_ref, iwm_ref = rest[pages_per_step:]
    g = pl.program_id(1)
    shape = (n_new, TILE)

    @pl.when(g == 0)
    def _():
        iq = iq_ref[0].astype(F32)
        iw = ikw_ref[0][:, IDX_DIM:IDX_DIM + IDX_HEADS] * IDX_SCALE
        iqm_ref[...] = jnp.concatenate(
            [iq[:, h * IDX_DIM:(h + 1) * IDX_DIM] for h in range(IDX_HEADS)], axis=0).astype(BF16)
        iwm_ref[...] = jnp.concatenate(
            [jnp.broadcast_to(iw[:, h:h + 1], shape) for h in range(IDX_HEADS)], axis=0)

    def scores(ik_bf):
        width = ik_bf.shape[0]
        x = lax.dot_general(iqm_ref[...], ik_bf, NT_DIMS, preferred_element_type=F32)
        y = jnp.maximum(x, 0.0) * jnp.concatenate([iwm_ref[...]] * (width // TILE), axis=1)
        return jnp.sum(y.reshape(IDX_HEADS, n_new, width), axis=0)

    ik = jnp.concatenate([page_refs[r][0, 0] for r in range(pages_per_step)], axis=0).astype(BF16)
    key = _sortable_key(scores(ik))
    for r in range(pages_per_step):
        skey_ref[g * pages_per_step + r, 0] = key[:, r * TILE:(r + 1) * TILE]

    @pl.when(g == pl.num_programs(1) - 1)
    def _():
        col = lax.broadcasted_iota(I32, shape, 1)
        row = lax.broadcasted_iota(I32, shape, 0)
        ik_new = ikw_ref[0][:, :IDX_DIM]
        ik_pad = jnp.concatenate([ik_new, jnp.zeros((TILE - n_new, IDX_DIM), F32)], axis=0).astype(BF16)
        skey_ref[n_pages, 0] = jnp.where((col <= row) & (col < n_new), _sortable_key(scores(ik_pad)), INT_MIN)


def _sample_scores(page_table, iq_s, ikw_s, cache_ik):
    bs, n_new, _ = iq_s.shape
    n_pages = page_table.shape[1]
    page = cache_ik.shape[2]
    assert page == TILE and cache_ik.shape[3] == IDX_DIM
    pps = min(8, n_pages)
    assert n_pages % pps == 0
    page_spec = lambda r: pl.BlockSpec(
        (1, 1, page, IDX_DIM), lambda b, g, pt, r=r: (0, pt[b, g * pps + r], 0, 0))
    grid_spec = pltpu.PrefetchScalarGridSpec(
        num_scalar_prefetch=1,
        grid=(bs, n_pages // pps),
        in_specs=[pl.BlockSpec((1, n_new, IDX_WIDTH), lambda b, g, pt: (b, 0, 0)),
                  pl.BlockSpec((1, n_new, LANES), lambda b, g, pt: (b, 0, 0))]
                 + [page_spec(r) for r in range(pps)],
        out_specs=pl.BlockSpec((n_pages + 1, 1, n_new, TILE), lambda b, g, pt: (0, b, 0, 0)),
        scratch_shapes=[pltpu.VMEM((IDX_HEADS * n_new, IDX_DIM), BF16),
                        pltpu.VMEM((IDX_HEADS * n_new, TILE), F32)],
    )
    return pl.pallas_call(
        functools.partial(_sample_scores_body, pages_per_step=pps, n_pages=n_pages, n_new=n_new),
        grid_spec=grid_spec,
        out_shape=jax.ShapeDtypeStruct((n_pages + 1, bs, n_new, TILE), I32),
        compiler_params=_params(("parallel", "arbitrary")),
        name="sample_scores",
    )(page_table, iq_s, ikw_s, *([cache_ik] * pps))


def _sample_select_body(skey_ref, selb_ref, *, ntile, rows, top_k):
    _select_top_k(skey_ref, selb_ref, ntile, (rows, TILE), top_k)


def _sample_select(skey, top_k):
    ntile, rows, _ = skey.shape
    rb = _row_block(rows, 256)
    blk = pl.BlockSpec((ntile, rb, TILE), lambda i: (0, i, 0))
    return pl.pallas_call(
        functools.partial(_sample_select_body, ntile=ntile, rows=rb, top_k=top_k),
        grid=(rows // rb,),
        in_specs=[blk],
        out_specs=blk,
        out_shape=jax.ShapeDtypeStruct((ntile, rows, TILE), F32),
        compiler_params=_params(("parallel",)),
        name="sample_select",
    )(skey)


def _sample_attn_body(pt_ref, q_ref, selp_ref, seln_ref, knew_ref, vnew_ref, expand_ref,
                      bias_ref, bnew_ref, *rest, pages_per_step, n_pages, n_new):
    k_refs = rest[:pages_per_step]
    v_refs = rest[pages_per_step:2 * pages_per_step]
    o_ref, qm_ref, m_ref, l_ref, acc_ref = rest[2 * pages_per_step:]
    g = pl.program_id(1)
    nrow = N_HEADS * n_new
    wide = TILE * N_HEADS

    @pl.when(g == 0)
    def _():
        q = q_ref[0].astype(F32)
        qm_ref[...] = jnp.concatenate(
            [q[:, h * HEAD_DIM:(h + 1) * HEAD_DIM] for h in range(N_HEADS)], axis=0).astype(BF16)
        m_ref[...] = jnp.full((nrow, LANES), MASK_VALUE, F32)
        l_ref[...] = jnp.zeros((nrow, LANES), F32)
        acc_ref[...] = jnp.zeros((nrow, HEAD_DIM), F32)

    def update(s, v_bf):
        m = m_ref[...]
        m_new = jnp.maximum(m, jnp.max(s, axis=-1, keepdims=True))
        alpha = jnp.exp(m - m_new)
        p = jnp.exp(s - m_new[:, :1])
        l_ref[...] = alpha * l_ref[...] + jnp.sum(p, axis=-1, keepdims=True)
        acc_ref[...] = alpha * acc_ref[...] + jnp.dot(p.astype(BF16), v_bf, preferred_element_type=F32)
        m_ref[...] = m_new

    def expand_sel(selb):
        x = jnp.dot(selb.astype(BF16), expand_ref[...], preferred_element_type=F32)
        return jnp.concatenate([x] * N_HEADS, axis=0)

    def flat(ref):
        return ref[0, 0].reshape(wide, HEAD_DIM)

    kp = jnp.concatenate([flat(k_refs[r]) for r in range(pages_per_step)], axis=0).astype(BF16)
    vp = jnp.concatenate([flat(v_refs[r]) for r in range(pages_per_step)], axis=0).astype(BF16)
    s = lax.dot_general(qm_ref[...], kp, NT_DIMS, preferred_element_type=F32) * ATTN_SCALE
    extra = []
    for r in range(pages_per_step):
        near = (g * pages_per_step + r == n_pages - 1).astype(I32)
        extra.append(bias_ref[near] + expand_sel(selp_ref[r]))
    update(s + jnp.concatenate(extra, axis=1), vp)

    @pl.when(g == pl.num_programs(1) - 1)
    def _():
        pad = jnp.zeros((TILE - n_new * N_HEADS, HEAD_DIM), F32)
        kn = jnp.concatenate([knew_ref[0].reshape(n_new * N_HEADS, HEAD_DIM), pad], axis=0).astype(BF16)
        vn = jnp.concatenate([vnew_ref[0].reshape(n_new * N_HEADS, HEAD_DIM), pad], axis=0).astype(BF16)
        s = lax.dot_general(qm_ref[...], kn, NT_DIMS, preferred_element_type=F32) * ATTN_SCALE
        s = s + bnew_ref[...] + expand_sel(seln_ref[0])[:, :TILE]
        update(s, vn)
        o_ref[0] = acc_ref[...] / l_ref[...]


def _sample_attn(page_table, q_s, selb, k_new, v_new, tables, cache_k, cache_v):
    bs, n_new, _ = q_s.shape
    n_pages = page_table.shape[1]
    page = cache_k.shape[2]
    assert page == TILE and cache_k.shape[3:] == (N_HEADS, HEAD_DIM)
    assert n_new * N_HEADS <= TILE and n_new == SUBLANES
    pps = min(4, n_pages)
    assert n_pages % pps == 0
    nrow = N_HEADS * n_new
    wide = TILE * N_HEADS
    kv_spec = lambda r: pl.BlockSpec(
        (1, 1, page, N_HEADS, HEAD_DIM), lambda b, g, pt, r=r: (0, pt[b, g * pps + r], 0, 0, 0))
    const = lambda shape: pl.BlockSpec(shape, lambda b, g, pt: (0,) * len(shape))
    per_seq = lambda shape: pl.BlockSpec((1,) + shape, lambda b, g, pt: (b,) + (0,) * len(shape))
    grid_spec = pltpu.PrefetchScalarGridSpec(
        num_scalar_prefetch=1,
        grid=(bs, n_pages // pps),
        in_specs=[per_seq((n_new, ATTN_WIDTH)),
                  pl.BlockSpec((pps, n_new, TILE), lambda b, g, pt: (g, b, 0)),
                  pl.BlockSpec((1, n_new, TILE), lambda b, g, pt: (n_pages, b, 0)),
                  per_seq((n_new, N_HEADS, HEAD_DIM)), per_seq((n_new, N_HEADS, HEAD_DIM)),
                  const((TILE, wide)), const((2, nrow, wide)), const((nrow, TILE))]
                 + [kv_spec(r) for r in range(pps)] + [kv_spec(r) for r in range(pps)],
        out_specs=per_seq((nrow, HEAD_DIM)),
        scratch_shapes=[pltpu.VMEM((nrow, HEAD_DIM), BF16), pltpu.VMEM((nrow, LANES), F32),
                        pltpu.VMEM((nrow, LANES), F32), pltpu.VMEM((nrow, HEAD_DIM), F32)],
    )
    return pl.pallas_call(
        functools.partial(_sample_attn_body, pages_per_step=pps, n_pages=n_pages, n_new=n_new),
        grid_spec=grid_spec,
        out_shape=jax.ShapeDtypeStruct((bs, nrow, HEAD_DIM), F32),
        compiler_params=_params(("parallel", "arbitrary")),
        name="sample_attn",
    )(page_table, q_s, selb, selb, k_new, v_new, tables["expand"], tables["bias"], tables["bnew"],
      *([cache_k] * pps), *([cache_v] * pps))


def _t5_bucket(rel):
    n = np.maximum(np.asarray(rel, np.int64), 0)
    max_exact = REL_BUCKETS // 2
    nf = np.maximum(n, 1).astype(np.float32)
    scaled = (np.log(nf / np.float32(max_exact)) / np.float32(math.log(REL_MAX_DIST / max_exact))
              * np.float32(REL_BUCKETS - max_exact))
    large = np.minimum(max_exact + scaled.astype(np.int32), REL_BUCKETS - 1)
    return np.where(n < max_exact, n, large).astype(np.int32)


assert np.all(_t5_bucket(np.arange(TILE + 1, 1 << 18)) == REL_BUCKETS - 1)


def _bias_lookup(rel_bias, bucket, head):
    rb = rel_bias.astype(F32)
    out = jnp.zeros(bucket.shape, F32)
    for bk in range(REL_BUCKETS):
        for h in range(N_HEADS):
            hit = (bucket == bk) & (head == h)
            if hit.any():
                out = jnp.where(jnp.asarray(hit), rb[bk, h], out)
    return out


def _prompt_bias_tables(rel_bias):
    h = np.arange(N_HEADS)[None, :, None, None]
    i = np.arange(TILE)[None, None, :, None]
    jj = np.arange(TILE)[None, None, None, :]
    rel3 = np.stack([i - jj, TILE + i - jj, np.full((1, 1, TILE, TILE), 2 * TILE)]).reshape(3, 1, TILE, TILE)
    rel3 = np.broadcast_to(rel3, (3, N_HEADS, TILE, TILE))
    bias3 = _bias_lookup(rel_bias, _t5_bucket(rel3), np.broadcast_to(h, rel3.shape))
    qi = np.arange(QB)[None, None, :, None]
    relm = np.stack([np.broadcast_to(N_META + qi - jj, (1, 1, QB, TILE)),
                     np.full((1, 1, QB, TILE), 2 * TILE)]).reshape(2, 1, QB, TILE)
    relm = np.broadcast_to(relm, (2, N_HEADS, QB, TILE))
    bmeta = _bias_lookup(rel_bias, _t5_bucket(relm), np.broadcast_to(h, relm.shape))
    return bias3, bmeta


def _sample_tables(rel_bias, n_new):
    nrow = N_HEADS * n_new
    wide = TILE * N_HEADS
    rh = np.broadcast_to(np.arange(nrow)[:, None] // n_new, (nrow, wide))
    rq = np.arange(nrow)[:, None] % n_new
    cs = np.arange(wide)[None, :] // N_HEADS
    ch = np.arange(wide)[None, :] % N_HEADS
    rel = np.stack([np.full((nrow, wide), 2 * TILE), TILE + rq - cs])
    bias = _bias_lookup(rel_bias, _t5_bucket(rel), np.broadcast_to(rh, rel.shape))
    bias = jnp.where(jnp.asarray(rh == ch), bias, MASK_VALUE)
    cj = np.arange(TILE)[None, :] // N_HEADS
    chn = np.arange(TILE)[None, :] % N_HEADS
    ok = (rh[:, :TILE] == chn) & (cj < n_new)
    bnew = _bias_lookup(rel_bias, _t5_bucket(np.clip(rq - cj, 0, None)), rh[:, :TILE])
    bnew = jnp.where(jnp.asarray(ok), bnew, MASK_VALUE)
    expand = np.zeros((TILE, wide), np.float32)
    expand[cs[0], np.arange(wide)] = 1.0
    return {"expand": jnp.asarray(expand, BF16), "bias": bias, "bnew": bnew}


def _split_w_in(w_in, d, c):
    sizes = (ATTN_WIDTH, ATTN_WIDTH, ATTN_WIDTH, IDX_WIDTH, IDX_DIM, IDX_HEADS, c, c, d, d)
    offs = np.concatenate([[0], np.cumsum(sizes)])
    assert offs[-1] == w_in.shape[1], (offs[-1], w_in.shape)
    names = ("q", "k", "v", "iq", "ik", "iw", "ca", "cb", "ga", "gb")
    parts = {n: w_in[:, offs[i]:offs[i + 1]].astype(BF16) for i, n in enumerate(names)}
    pad = jnp.zeros((d, LANES - IDX_DIM - IDX_HEADS), BF16)
    parts["ikw"] = jnp.concatenate([parts.pop("ik"), parts.pop("iw"), pad], axis=1)
    return parts


def _pad_rows(a, rows):
    return jnp.concatenate([a, jnp.zeros((rows - a.shape[0],) + a.shape[1:], a.dtype)], axis=0)


def kernel(x_prompt, x_sample, cache_k, cache_v, cache_ik, state_conv, page_table, meta_tokens, rel_bias,
           norm_mix_g, w_in, q_norm_g, k_norm_g, w_attn_br, w_dw, b_dw, ln_conv_g, ln_conv_b, w_conv_br,
           w_out, norm_ffn_g, w_ffn_gate, w_ffn_up, w_ffn_down):
    assert w_in.shape[0] == 1, "single trunk layer"
    b, seq, d = x_prompt.shape
    bs, n_new, _ = x_sample.shape
    c = w_dw.shape[2]
    n_pages = page_table.shape[1]
    past_len = n_pages * cache_k.shape[2]
    assert seq % QB == 0 and seq >= CONV_CTX and N_META + CONV_CTX <= 2 * HALO and N_META <= HALO
    t_real = seq + N_META

    w = _split_w_in(w_in[0], d, c)
    g_mix = norm_mix_g[0][None]
    g_ffn = norm_ffn_g[0][None]
    qg = q_norm_g[0][None]
    kg = k_norm_g[0][None]
    wa = w_attn_br[0].astype(BF16)
    wcb = w_conv_br[0].astype(BF16)
    wo = w_out[0].astype(BF16)
    wg = w_ffn_gate[0].astype(BF16)
    wu = w_ffn_up[0].astype(BF16)
    wd = w_ffn_down[0].astype(BF16)
    conv_w = (w_dw[0], b_dw[0][None], ln_conv_g[0][None], ln_conv_b[0][None])

    def tail(h2d, ao2d, yc2d):
        m = _gate(h2d, g_mix, ao2d, yc2d, w["ga"], w["gb"], wa, wcb)
        return _ffn(_outproj(h2d, m, wo), g_ffn, wg, wu, wd)

    _, kf_m, kb_m, vf_m, vb_m, _, u_m, ikw_m = _inproj(meta_tokens.astype(F32), g_mix, w, qg, kg)

    hp = x_prompt.reshape(b * seq, d)
    q, kf, kb, vf, vb, iq, u, ikw = _inproj(hp, g_mix, w, qg, kg)
    r3 = lambda a: a.reshape(b, seq, a.shape[-1])
    bias3, bmeta = _prompt_bias_tables(rel_bias)
    ao = _prompt_attn(r3(q), r3(iq), r3(ikw), r3(kb), r3(vb), _pad_rows(kb_m, TILE), _pad_rows(vb_m, TILE),
                      _pad_rows(ikw_m, TILE), bias3, bmeta, min(TOPK_MAX, seq // 4))
    halo_p = jnp.concatenate([jnp.zeros((HALO - N_META, c), F32), u_m], axis=0)[None]
    yc = _conv(r3(u), halo_p, *conv_w)
    y_prompt = tail(hp, ao.reshape(b * seq, ATTN_WIDTH), yc.reshape(b * seq, c)).reshape(b, seq, d)
    with_meta = lambda m_rows, rows: jnp.concatenate(
        [jnp.broadcast_to(m_rows[None], (b,) + m_rows.shape), r3(rows)], axis=1)
    k_prompt = with_meta(kf_m, kf).reshape(1, b, t_real, N_HEADS, HEAD_DIM)
    v_prompt = with_meta(vf_m, vf).reshape(1, b, t_real, N_HEADS, HEAD_DIM)
    ik_prompt = with_meta(ikw_m[:, :IDX_DIM], ikw[:, :IDX_DIM])[None]
    conv_prompt = r3(u)[None, :, seq - CONV_CTX:]

    hs = x_sample.reshape(bs * n_new, d)
    q, kf, kb, vf, vb, iq, u, ikw = _inproj(hs, g_mix, w, qg, kg)
    s3 = lambda a: a.reshape(bs, n_new, a.shape[-1])
    s4 = lambda a: a.reshape(bs, n_new, N_HEADS, HEAD_DIM)
    top_k = min(TOPK_MAX, (past_len + n_new) // 4)
    skey = _sample_scores(page_table, s3(iq), s3(ikw), cache_ik)
    selb = _sample_select(skey.reshape(n_pages + 1, bs * n_new, TILE), top_k)
    ao_s = _sample_attn(page_table, s3(q), selb, s4(kf), s4(vf), _sample_tables(rel_bias, n_new),
                        cache_k, cache_v)
    ao_s = ao_s.reshape(bs, N_HEADS, n_new, HEAD_DIM).transpose(0, 2, 1, 3).reshape(bs * n_new, ATTN_WIDTH)
    state = state_conv[0].astype(F32)
    halo_s = jnp.concatenate([jnp.zeros((bs, HALO - CONV_CTX, c), F32), state], axis=1)
    yc_s = _conv(s3(u), halo_s, *conv_w).reshape(bs * n_new, c)
    y_sample = tail(hs, ao_s.astype(BF16), yc_s).reshape(bs, n_new, d)
    k_sample = s4(kf)[None]
    v_sample = s4(vf)[None]
    ik_sample = s3(ikw)[None, :, :, :IDX_DIM]
    conv_sample = jnp.concatenate([state, s3(u)], axis=1)[None, :, n_new:]

    return (y_prompt, y_sample, k_prompt, v_prompt, ik_prompt, conv_prompt,
            k_sample, v_sample, ik_sample, conv_sample)
```

```python
import functools
import math

import numpy as np
import jax
import jax.numpy as jnp
from jax import lax
from jax.experimental import pallas as pl
from jax.experimental.pallas import tpu as pltpu

N_META = 16
N_HEADS = 8
HEAD_DIM = 128
ATTN_WIDTH = N_HEADS * HEAD_DIM
IDX_HEADS = 16
IDX_DIM = 64
IDX_WIDTH = IDX_HEADS * IDX_DIM
TOPK_MAX = 256
REL_BUCKETS = 32
REL_MAX_DIST = 128
CONV_WIDTH = 31
CONV_CTX = CONV_WIDTH - 1
NORM_EPS = 1e-6
MASK_VALUE = -1e30

LANES = 128
SUBLANES = 8
TILE = LANES
QB = 2 * TILE
HALO = 32
INT_MIN = -2 ** 31
VMEM_LIMIT = 52 * 1024 * 1024
ROW_BLOCK_CAP = 512

F32 = jnp.float32
BF16 = jnp.bfloat16
I32 = jnp.int32
NT_DIMS = (((1,), (1,)), ((), ()))
IDX_SCALE = IDX_DIM ** -0.5 * IDX_HEADS ** -0.5
ATTN_SCALE = HEAD_DIM ** -0.5


def _row_block(rows, cap=ROW_BLOCK_CAP):
    best = None
    for d in range(SUBLANES, min(rows, cap) + 1, SUBLANES):
        if rows % d == 0:
            best = d
    assert best is not None, rows
    return best


def _params(sem):
    return pltpu.CompilerParams(dimension_semantics=sem, vmem_limit_bytes=VMEM_LIMIT)


def _rms(x, g):
    ms = jnp.mean(x * x, axis=-1, keepdims=True)
    return x * lax.rsqrt(ms + NORM_EPS) * g


def _sortable_key(s):
    s = jnp.where(s == 0.0, 0.0, s)
    bits = lax.bitcast_convert_type(s, I32)
    return jnp.where(bits < 0, bits ^ jnp.int32(0x7FFFFFFF), bits)


def _inproj_body(h_ref, g_ref, wq_ref, wk_ref, wv_ref, wiq_ref, wa_ref, wb_ref, wikw_ref, qg_ref, kg_ref,
                 q_ref, kf_ref, kb_ref, vf_ref, vb_ref, iq_ref, u_ref, ikw_ref, xn_ref, *, tn):
    j = pl.program_id(1)

    @pl.when(j == 0)
    def _():
        xn_ref[...] = _rms(h_ref[...], g_ref[...]).astype(BF16)
        ikw_ref[...] = jnp.dot(xn_ref[...], wikw_ref[...], preferred_element_type=F32)

    xn = xn_ref[...]

    def mm(w_ref):
        return jnp.dot(xn, w_ref[...], preferred_element_type=F32)

    def head_norm(z, g):
        cols = [_rms(z[:, c * HEAD_DIM:(c + 1) * HEAD_DIM], g) for c in range(tn // HEAD_DIM)]
        return jnp.concatenate(cols, axis=-1)

    q_ref[...] = head_norm(mm(wq_ref), qg_ref[...]).astype(BF16)
    k = head_norm(mm(wk_ref), kg_ref[...])
    kf_ref[...] = k
    kb_ref[...] = k.astype(BF16)
    v = mm(wv_ref)
    vf_ref[...] = v
    vb_ref[...] = v.astype(BF16)
    iq_ref[...] = mm(wiq_ref).astype(BF16)
    u_ref[...] = mm(wa_ref) * jax.nn.sigmoid(mm(wb_ref))


def _inproj(h2d, g, w, qg, kg):
    rows, d = h2d.shape
    c = w["ca"].shape[1]
    assert c == ATTN_WIDTH, "fused input projection assumes CONV_CH == ATTN_WIDTH"
    tm = _row_block(rows)
    tn = 256
    nj = ATTN_WIDTH // tn
    row_spec = lambda width: pl.BlockSpec((tm, width), lambda i, j: (i, j))
    w_spec = pl.BlockSpec((d, tn), lambda i, j: (0, j))
    const = lambda shape: pl.BlockSpec(shape, lambda i, j: (0, 0))
    out_shape = (
        jax.ShapeDtypeStruct((rows, ATTN_WIDTH), BF16),
        jax.ShapeDtypeStruct((rows, ATTN_WIDTH), F32),
        jax.ShapeDtypeStruct((rows, ATTN_WIDTH), BF16),
        jax.ShapeDtypeStruct((rows, ATTN_WIDTH), F32),
        jax.ShapeDtypeStruct((rows, ATTN_WIDTH), BF16),
        jax.ShapeDtypeStruct((rows, IDX_WIDTH), BF16),
        jax.ShapeDtypeStruct((rows, c), F32),
        jax.ShapeDtypeStruct((rows, LANES), F32),
    )
    return pl.pallas_call(
        functools.partial(_inproj_body, tn=tn),
        grid=(rows // tm, nj),
        in_specs=[pl.BlockSpec((tm, d), lambda i, j: (i, 0)), const((1, d)),
                  w_spec, w_spec, w_spec, w_spec, w_spec, w_spec, const((d, LANES)),
                  const((1, HEAD_DIM)), const((1, HEAD_DIM))],
        out_specs=[row_spec(tn)] * 7 + [pl.BlockSpec((tm, LANES), lambda i, j: (i, 0))],
        out_shape=out_shape,
        scratch_shapes=[pltpu.VMEM((tm, d), BF16)],
        compiler_params=_params(("parallel", "arbitrary")),
        name="inproj",
    )(h2d, g, w["q"], w["k"], w["v"], w["iq"], w["ca"], w["cb"], w["ikw"], qg, kg)


def _conv_body(x_ref, halo_ref, w_ref, b_ref, lg_ref, lb_ref, o_ref, buf_ref, y_ref, *, tt, c):
    @pl.when(pl.program_id(1) == 0)
    def _():
        buf_ref[0:HALO, :] = halo_ref[0]

    buf_ref[HALO:HALO + tt, :] = x_ref[0]
    lead = HALO - CONV_CTX
    rsum = jnp.zeros((tt, LANES), F32)
    for cc in range(c // LANES):
        cs = slice(cc * LANES, (cc + 1) * LANES)
        acc = jnp.zeros((tt, LANES), F32) + b_ref[:, cs]
        for j in range(CONV_WIDTH):
            acc = acc + w_ref[j:j + 1, cs] * buf_ref[lead + j:lead + j + tt, cs]
        y_ref[:, cs] = acc
        rsum = rsum + acc
    mu = jnp.sum(rsum, axis=-1, keepdims=True) * (1.0 / c)
    vsum = jnp.zeros((tt, LANES), F32)
    for cc in range(c // LANES):
        cs = slice(cc * LANES, (cc + 1) * LANES)
        dlt = y_ref[:, cs] - mu
        vsum = vsum + dlt * dlt
    inv = lax.rsqrt(jnp.sum(vsum, axis=-1, keepdims=True) * (1.0 / c) + NORM_EPS)
    for cc in range(c // LANES):
        cs = slice(cc * LANES, (cc + 1) * LANES)
        z = (y_ref[:, cs] - mu) * inv * lg_ref[:, cs] + lb_ref[:, cs]
        o_ref[0, :, cs] = (z * jax.nn.sigmoid(z)).astype(BF16)
    buf_ref[0:HALO, :] = buf_ref[tt:tt + HALO, :]


def _conv(x, halo, w_dw, b_dw, ln_g, ln_b):
    n, t, c = x.shape
    tt = _row_block(t, 128)
    halo_map = (lambda b, i: (b, 0, 0)) if halo.shape[0] == n else (lambda b, i: (0, 0, 0))
    const = lambda shape: pl.BlockSpec(shape, lambda b, i: (0, 0))
    blk = pl.BlockSpec((1, tt, c), lambda b, i: (b, i, 0))
    return pl.pallas_call(
        functools.partial(_conv_body, tt=tt, c=c),
        grid=(n, t // tt),
        in_specs=[blk, pl.BlockSpec((1, HALO, c), halo_map),
                  const((CONV_WIDTH, c)), const((1, c)), const((1, c)), const((1, c))],
        out_specs=blk,
        out_shape=jax.ShapeDtypeStruct((n, t, c), BF16),
        scratch_shapes=[pltpu.VMEM((HALO + tt, c), F32), pltpu.VMEM((tt, c), F32)],
        compiler_params=_params(("parallel", "arbitrary")),
        name="conv",
    )(x, halo, w_dw, b_dw, ln_g, ln_b)


def _gate_body(h_ref, g_ref, ao_ref, yc_ref, wga_ref, wgb_ref, wa_ref, wcb_ref, m_ref, xn_ref):
    @pl.when(pl.program_id(1) == 0)
    def _():
        xn_ref[...] = _rms(h_ref[...], g_ref[...]).astype(BF16)

    xn = xn_ref[...]
    ga = jnp.dot(xn, wga_ref[...], preferred_element_type=F32)
    gb = jnp.dot(xn, wgb_ref[...], preferred_element_type=F32)
    a = jnp.dot(ao_ref[...], wa_ref[...], preferred_element_type=F32)
    co = jnp.dot(yc_ref[...], wcb_ref[...], preferred_element_type=F32)
    m_ref[...] = (jax.nn.sigmoid(ga) * a + jax.nn.sigmoid(gb) * co).astype(BF16)


def _gate(h2d, g, ao, yc, wga, wgb, wa, wcb):
    rows, d = h2d.shape
    tm = _row_block(rows)
    tn = 256
    full_rows = lambda width: pl.BlockSpec((tm, width), lambda i, j: (i, 0))
    wcol = lambda k: pl.BlockSpec((k, tn), lambda i, j: (0, j))
    return pl.pallas_call(
        _gate_body,
        grid=(rows // tm, d // tn),
        in_specs=[full_rows(d), pl.BlockSpec((1, d), lambda i, j: (0, 0)),
                  full_rows(ao.shape[1]), full_rows(yc.shape[1]),
                  wcol(d), wcol(d), wcol(wa.shape[0]), wcol(wcb.shape[0])],
        out_specs=pl.BlockSpec((tm, tn), lambda i, j: (i, j)),
        out_shape=jax.ShapeDtypeStruct((rows, d), BF16),
        scratch_shapes=[pltpu.VMEM((tm, d), BF16)],
        compiler_params=_params(("parallel", "arbitrary")),
        name="gate",
    )(h2d, g, ao, yc, wga, wgb, wa, wcb)


def _outproj_body(h_ref, m_ref, w_ref, o_ref):
    o_ref[...] = h_ref[...] + jnp.dot(m_ref[...], w_ref[...], preferred_element_type=F32)


def _outproj(h2d, m, w_out):
    rows, d = h2d.shape
    tm = _row_block(rows)
    tn = min(512, d)
    return pl.pallas_call(
        _outproj_body,
        grid=(rows // tm, d // tn),
        in_specs=[pl.BlockSpec((tm, tn), lambda i, j: (i, j)),
                  pl.BlockSpec((tm, d), lambda i, j: (i, 0)),
                  pl.BlockSpec((d, tn), lambda i, j: (0, j))],
        out_specs=pl.BlockSpec((tm, tn), lambda i, j: (i, j)),
        out_shape=jax.ShapeDtypeStruct((rows, d), F32),
        compiler_params=_params(("parallel", "arbitrary")),
        name="outproj",
    )(h2d, m, w_out)


def _ffn_body(h_ref, g_ref, wg_ref, wu_ref, wd_ref, o_ref, hn_ref):
    @pl.when(pl.program_id(1) == 0)
    def _():
        h = h_ref[...]
        hn_ref[...] = _rms(h, g_ref[...]).astype(BF16)
        o_ref[...] = h

    hn = hn_ref[...]
    gt = jnp.dot(hn, wg_ref[...], preferred_element_type=F32)
    up = jnp.dot(hn, wu_ref[...], preferred_element_type=F32)
    act = (gt * jax.nn.sigmoid(gt) * up).astype(BF16)
    o_ref[...] += jnp.dot(act, wd_ref[...], preferred_element_type=F32)


def _ffn(h2d, g, wg, wu, wd):
    rows, d = h2d.shape
    f = wg.shape[1]
    tm = _row_block(rows)
    tf = 512
    assert f % tf == 0, f
    return pl.pallas_call(
        _ffn_body,
        grid=(rows // tm, f // tf),
        in_specs=[pl.BlockSpec((tm, d), lambda i, j: (i, 0)), pl.BlockSpec((1, d), lambda i, j: (0, 0)),
                  pl.BlockSpec((d, tf), lambda i, j: (0, j)), pl.BlockSpec((d, tf), lambda i, j: (0, j)),
                  pl.BlockSpec((tf, d), lambda i, j: (j, 0))],
        out_specs=pl.BlockSpec((tm, d), lambda i, j: (i, 0)),
        out_shape=jax.ShapeDtypeStruct((rows, d), F32),
        scratch_shapes=[pltpu.VMEM((tm, d), BF16)],
        compiler_params=_params(("parallel", "arbitrary")),
        name="ffn",
    )(h2d, g, wg, wu, wd)


def _select_top_k(keyt_ref, selt_ref, ntile, ncol, top_k):
    groups = TILE // SUBLANES

    def tile_rows(t):
        return pl.ds(pl.multiple_of(t * TILE, TILE), TILE)

    def count_where(pred):
        def body(t, c):
            hit = jnp.where(pred(keyt_ref[tile_rows(t), :]), 1, 0)
            return c + jnp.sum(hit.reshape(groups, SUBLANES, ncol), axis=0)
        c = lax.fori_loop(0, ntile, body, jnp.zeros((SUBLANES, ncol), I32))
        return jnp.sum(c, axis=0, keepdims=True)

    n_pos = count_where(lambda key: key >= 0)
    thr0 = jnp.where(n_pos >= top_k, 0, INT_MIN).astype(I32)

    def step(s, carry):
        thr, n_thr = carry
        cand = thr | (jnp.int32(1) << (30 - s))
        n = count_where(lambda key: key >= cand)
        ok = n >= top_k
        return jnp.where(ok, cand, thr), jnp.where(ok, n, n_thr)

    thr, n_thr = lax.fori_loop(0, 31, step, (thr0, jnp.where(n_pos >= top_k, n_pos, 0)))
    ties_matter = jnp.max(n_thr.astype(F32)) > top_k

    @pl.when(jnp.logical_not(ties_matter))
    def _():
        def body(t, carry):
            key = keyt_ref[tile_rows(t), :]
            selt_ref[tile_rows(t), :] = jnp.where((key >= thr) & (key != INT_MIN), 0.0, MASK_VALUE)
            return carry
        lax.fori_loop(0, ntile, body, 0)

    @pl.when(ties_matter)
    def _():
        room = (top_k - count_where(lambda key: key > thr)).astype(F32)
        sq = (TILE, TILE)
        tri = (lax.broadcasted_iota(I32, sq, 1) <= lax.broadcasted_iota(I32, sq, 0)).astype(BF16)

        def body(t, before):
            key = keyt_ref[tile_rows(t), :]
            eq = (key == thr) & (key != INT_MIN)
            eqf = jnp.where(eq, 1.0, 0.0)
            rank = before + jnp.dot(tri, eqf.astype(BF16), preferred_element_type=F32)
            sel = (key > thr) | (eq & (rank <= room))
            selt_ref[tile_rows(t), :] = jnp.where(sel, 0.0, MASK_VALUE)
            return before + jnp.sum(eqf, axis=0, keepdims=True)

        lax.fori_loop(0, ntile, body, jnp.zeros((1, ncol), F32))


def _prompt_attn_body(q_ref, iq_ref, ikwq_ref, ikw_ref, k_ref, v_ref, kmeta_ref, vmeta_ref, ikwm_ref,
                      bias_ref, bmeta_ref, o_ref,
                      iqh_ref, iwt_ref, keyt_ref, selt_ref, m_ref, l_ref, acc_ref, *, top_k):
    j = pl.program_id(1)
    nchunk = j + 1
    shape = (QB, TILE)
    krow = lax.broadcasted_iota(I32, (QB, QB), 0)
    qcol = lax.broadcasted_iota(I32, (QB, QB), 1)

    iq = iq_ref[0]
    for h in range(IDX_HEADS):
        iqh_ref[h] = iq[:, h * IDX_DIM:(h + 1) * IDX_DIM]
    iwt_ref[...] = ikwq_ref[0].T[IDX_DIM:IDX_DIM + IDX_HEADS, :] * IDX_SCALE

    def index_scores_t(ik_bf):
        acc = jnp.zeros((ik_bf.shape[0], QB), F32)
        for h in range(IDX_HEADS):
            x = lax.dot_general(ik_bf, iqh_ref[h], NT_DIMS, preferred_element_type=F32)
            acc = acc + jnp.maximum(x, 0.0) * iwt_ref[h:h + 1, :]
        return acc

    s_meta = index_scores_t(ikwm_ref[:, :IDX_DIM].astype(BF16))
    is_meta = lax.broadcasted_iota(I32, (TILE, QB), 0) < N_META
    keyt_ref[0:TILE, :] = jnp.where(is_meta, _sortable_key(s_meta), INT_MIN)

    def score_chunk(c, carry):
        off = pl.multiple_of(c * QB, QB)
        ik = ikw_ref[0, pl.ds(off, QB), :][:, :IDX_DIM].astype(BF16)
        causal = c * QB + krow <= j * QB + qcol
        keyt_ref[pl.ds(pl.multiple_of(TILE + c * QB, TILE), QB), :] = jnp.where(
            causal, _sortable_key(index_scores_t(ik)), INT_MIN)
        return carry

    lax.fori_loop(0, nchunk, score_chunk, 0)

    _select_top_k(keyt_ref, selt_ref, 1 + 2 * nchunk, QB, top_k)

    far_meta = jnp.minimum(j, 1)
    selb = selt_ref[0:TILE, :].T
    for h in range(N_HEADS):
        hs = slice(h * HEAD_DIM, (h + 1) * HEAD_DIM)
        s = lax.dot_general(q_ref[0, :, hs], kmeta_ref[:, hs], NT_DIMS, preferred_element_type=F32) * ATTN_SCALE
        s = s + bmeta_ref[far_meta, h] + selb
        m = jnp.max(s, axis=-1, keepdims=True)
        p = jnp.exp(s - m)
        m_ref[h] = jnp.broadcast_to(m, shape)
        l_ref[h] = jnp.broadcast_to(jnp.sum(p, axis=-1, keepdims=True), shape)
        acc_ref[:, hs] = jnp.dot(p.astype(BF16), vmeta_ref[:, hs], preferred_element_type=F32)

    def attend_chunk(c, carry):
        off = pl.multiple_of(c * QB, QB)
        selb = selt_ref[pl.ds(pl.multiple_of(TILE + c * QB, TILE), QB), :].T
        dist = 2 * (j - c)
        idx = [[jnp.clip(dist + r - e, 0, 2) for e in range(2)] for r in range(2)]
        for h in range(N_HEADS):
            hs = slice(h * HEAD_DIM, (h + 1) * HEAD_DIM)
            kh = k_ref[0, pl.ds(off, QB), hs]
            vh = v_ref[0, pl.ds(off, QB), hs]
            s = lax.dot_general(q_ref[0, :, hs], kh, NT_DIMS, preferred_element_type=F32) * ATTN_SCALE
            bias = jnp.concatenate(
                [jnp.concatenate([bias_ref[idx[r][0], h], bias_ref[idx[r][1], h]], axis=1) for r in range(2)],
                axis=0)
            s = s + bias + selb
            m_old = m_ref[h]
            m_new = jnp.maximum(m_old, jnp.max(s, axis=-1, keepdims=True))
            alpha = jnp.exp(m_old - m_new)
            p = jnp.exp(s - jnp.concatenate([m_new, m_new], axis=1))
            l_ref[h] = alpha * l_ref[h] + jnp.sum(p, axis=-1, keepdims=True)
            acc_ref[:, hs] = alpha * acc_ref[:, hs] + jnp.dot(p.astype(BF16), vh, preferred_element_type=F32)
            m_ref[h] = m_new
        return carry

    lax.fori_loop(0, nchunk, attend_chunk, 0)

    for h in range(N_HEADS):
        hs = slice(h * HEAD_DIM, (h + 1) * HEAD_DIM)
        o_ref[0, :, hs] = (acc_ref[:, hs] / l_ref[h]).astype(BF16)


def _prompt_attn(q, iq, ikw, kb, vb, kmeta, vmeta, ikwm, bias3, bmeta, top_k):
    b, t, _ = q.shape
    nq = t // QB
    key_rows = TILE + t
    qblk = lambda width: pl.BlockSpec((1, QB, width), lambda bi, qi: (bi, qi, 0))
    seq = lambda width: pl.BlockSpec((1, t, width), lambda bi, qi: (bi, 0, 0))
    const = lambda shape: pl.BlockSpec(shape, lambda bi, qi: (0,) * len(shape))
    return pl.pallas_call(
        functools.partial(_prompt_attn_body, top_k=top_k),
        grid=(b, nq),
        in_specs=[qblk(ATTN_WIDTH), qblk(IDX_WIDTH), qblk(LANES), seq(LANES), seq(ATTN_WIDTH), seq(ATTN_WIDTH),
                  const((TILE, ATTN_WIDTH)), const((TILE, ATTN_WIDTH)), const((TILE, LANES)),
                  const((3, N_HEADS, TILE, TILE)), const((2, N_HEADS, QB, TILE))],
        out_specs=qblk(ATTN_WIDTH),
        out_shape=jax.ShapeDtypeStruct((b, t, ATTN_WIDTH), BF16),
        scratch_shapes=[pltpu.VMEM((IDX_HEADS, QB, IDX_DIM), BF16),
                        pltpu.VMEM((IDX_HEADS, QB), F32),
                        pltpu.VMEM((key_rows, QB), I32),
                        pltpu.VMEM((key_rows, QB), F32),
                        pltpu.VMEM((N_HEADS, QB, TILE), F32),
                        pltpu.VMEM((N_HEADS, QB, TILE), F32),
                        pltpu.VMEM((QB, ATTN_WIDTH), F32)],
        compiler_params=_params(("parallel", "arbitrary")),
        name="prompt_attn",
    )(q, iq, ikw, ikw, kb, vb, kmeta, vmeta, ikwm, bias3, bmeta)


def _sample_scores_body(pt_ref, iq_ref, ikw_ref, *rest, pages_per_step, n_pages, n_new):
    page_refs = rest[:pages_per_step]
    skey_ref, iqm_ref, iwm_ref = rest[pages_per_step:]
    g = pl.program_id(1)
    shape = (n_new, TILE)

    @pl.when(g == 0)
    def _():
        iq = iq_ref[0].astype(F32)
        iw = ikw_ref[0][:, IDX_DIM:IDX_DIM + IDX_HEADS] * IDX_SCALE
        iqm_ref[...] = jnp.concatenate(
            [iq[:, h * IDX_DIM:(h + 1) * IDX_DIM] for h in range(IDX_HEADS)], axis=0).astype(BF16)
        iwm_ref[...] = jnp.concatenate(
            [jnp.broadcast_to(iw[:, h:h + 1], shape) for h in range(IDX_HEADS)], axis=0)

    def scores(ik_bf):
        width = ik_bf.shape[0]
        x = lax.dot_general(iqm_ref[...], ik_bf, NT_DIMS, preferred_element_type=F32)
        y = jnp.maximum(x, 0.0) * jnp.concatenate([iwm_ref[...]] * (width // TILE), axis=1)
        return jnp.sum(y.reshape(IDX_HEADS, n_new, width), axis=0)

    ik = jnp.concatenate([page_refs[r][0, 0] for r in range(pages_per_step)], axis=0).astype(BF16)
    key = _sortable_key(scores(ik))
    for r in range(pages_per_step):
        skey_ref[g * pages_per_step + r, 0] = key[:, r * TILE:(r + 1) * TILE]

    @pl.when(g == pl.num_programs(1) - 1)
    def _():
        col = lax.broadcasted_iota(I32, shape, 1)
        row = lax.broadcasted_iota(I32, shape, 0)
        ik_new = ikw_ref[0][:, :IDX_DIM]
        ik_pad = jnp.concatenate([ik_new, jnp.zeros((TILE - n_new, IDX_DIM), F32)], axis=0).astype(BF16)
        skey_ref[n_pages, 0] = jnp.where((col <= row) & (col < n_new), _sortable_key(scores(ik_pad)), INT_MIN)


def _sample_scores(page_table, iq_s, ikw_s, cache_ik):
    bs, n_new, _ = iq_s.shape
    n_pages = page_table.shape[1]
    page = cache_ik.shape[2]
    assert page == TILE and cache_ik.shape[3] == IDX_DIM
    pps = min(8, n_pages)
    assert n_pages % pps == 0
    page_spec = lambda r: pl.BlockSpec(
        (1, 1, page, IDX_DIM), lambda b, g, pt, r=r: (0, pt[b, g * pps + r], 0, 0))
    grid_spec = pltpu.PrefetchScalarGridSpec(
        num_scalar_prefetch=1,
        grid=(bs, n_pages // pps),
        in_specs=[pl.BlockSpec((1, n_new, IDX_WIDTH), lambda b, g, pt: (b, 0, 0)),
                  pl.BlockSpec((1, n_new, LANES), lambda b, g, pt: (b, 0, 0))]
                 + [page_spec(r) for r in range(pps)],
        out_specs=pl.BlockSpec((n_pages + 1, 1, n_new, TILE), lambda b, g, pt: (0, b, 0, 0)),
        scratch_shapes=[pltpu.VMEM((IDX_HEADS * n_new, IDX_DIM), BF16),
                        pltpu.VMEM((IDX_HEADS * n_new, TILE), F32)],
    )
    return pl.pallas_call(
        functools.partial(_sample_scores_body, pages_per_step=pps, n_pages=n_pages, n_new=n_new),
        grid_spec=grid_spec,
        out_shape=jax.ShapeDtypeStruct((n_pages + 1, bs, n_new, TILE), I32),
        compiler_params=_params(("parallel", "arbitrary")),
        name="sample_scores",
    )(page_table, iq_s, ikw_s, *([cache_ik] * pps))


def _sample_select_body(skey_ref, selb_ref, keyt_ref, selt_ref, *, ntile, rows, top_k):
    for t in range(ntile):
        keyt_ref[t * TILE:(t + 1) * TILE, :] = skey_ref[t].T
    _select_top_k(keyt_ref, selt_ref, ntile, rows, top_k)
    for t in range(ntile):
        selb_ref[t] = selt_ref[t * TILE:(t + 1) * TILE, :].T


def _sample_select(skey, top_k):
    ntile, rows, _ = skey.shape
    rb = 2 * TILE if rows % (2 * TILE) == 0 else TILE
    assert rows % rb == 0, rows
    blk = pl.BlockSpec((ntile, rb, TILE), lambda i: (0, i, 0))
    return pl.pallas_call(
        functools.partial(_sample_select_body, ntile=ntile, rows=rb, top_k=top_k),
        grid=(rows // rb,),
        in_specs=[blk],
        out_specs=blk,
        out_shape=jax.ShapeDtypeStruct((ntile, rows, TILE), F32),
        scratch_shapes=[pltpu.VMEM((ntile * TILE, rb), I32), pltpu.VMEM((ntile * TILE, rb), F32)],
        compiler_params=_params(("parallel",)),
        name="sample_select",
    )(skey)


def _sample_attn_body(pt_ref, q_ref, selp_ref, seln_ref, knew_ref, vnew_ref, expand_ref,
                      bias_ref, bnew_ref, *rest, pages_per_step, n_pages, n_new):
    k_refs = rest[:pages_per_step]
    v_refs = rest[pages_per_step:2 * pages_per_step]
    o_ref, qm_ref, m_ref, l_ref, acc_ref = rest[2 * pages_per_step:]
    g = pl.program_id(1)
    nrow = N_HEADS * n_new
    wide = TILE * N_HEADS

    @pl.when(g == 0)
    def _():
        q = q_ref[0].astype(F32)
        qm_ref[...] = jnp.concatenate(
            [q[:, h * HEAD_DIM:(h + 1) * HEAD_DIM] for h in range(N_HEADS)], axis=0).astype(BF16)
        m_ref[...] = jnp.full((nrow, LANES), MASK_VALUE, F32)
        l_ref[...] = jnp.zeros((nrow, LANES), F32)
        acc_ref[...] = jnp.zeros((nrow, HEAD_DIM), F32)

    def update(s, v_bf):
        m = m_ref[...]
        m_new = jnp.maximum(m, jnp.max(s, axis=-1, keepdims=True))
        alpha = jnp.exp(m - m_new)
        p = jnp.exp(s - m_new[:, :1])
        l_ref[...] = alpha * l_ref[...] + jnp.sum(p, axis=-1, keepdims=True)
        acc_ref[...] = alpha * acc_ref[...] + jnp.dot(p.astype(BF16), v_bf, preferred_element_type=F32)
        m_ref[...] = m_new

    def expand_sel(selb):
        x = jnp.dot(selb.astype(BF16), expand_ref[...], preferred_element_type=F32)
        return jnp.concatenate([x] * N_HEADS, axis=0)

    def flat(ref):
        return ref[0, 0].reshape(wide, HEAD_DIM)

    kp = jnp.concatenate([flat(k_refs[r]) for r in range(pages_per_step)], axis=0).astype(BF16)
    vp = jnp.concatenate([flat(v_refs[r]) for r in range(pages_per_step)], axis=0).astype(BF16)
    s = lax.dot_general(qm_ref[...], kp, NT_DIMS, preferred_element_type=F32) * ATTN_SCALE
    extra = []
    for r in range(pages_per_step):
        near = (g * pages_per_step + r == n_pages - 1).astype(I32)
        extra.append(bias_ref[near] + expand_sel(selp_ref[r]))
    update(s + jnp.concatenate(extra, axis=1), vp)

    @pl.when(g == pl.num_programs(1) - 1)
    def _():
        pad = jnp.zeros((TILE - n_new * N_HEADS, HEAD_DIM), F32)
        kn = jnp.concatenate([knew_ref[0].reshape(n_new * N_HEADS, HEAD_DIM), pad], axis=0).astype(BF16)
        vn = jnp.concatenate([vnew_ref[0].reshape(n_new * N_HEADS, HEAD_DIM), pad], axis=0).astype(BF16)
        s = lax.dot_general(qm_ref[...], kn, NT_DIMS, preferred_element_type=F32) * ATTN_SCALE
        s = s + bnew_ref[...] + expand_sel(seln_ref[0])[:, :TILE]
        update(s, vn)
        o_ref[0] = acc_ref[...] / l_ref[...]


def _sample_attn(page_table, q_s, selb, k_new, v_new, tables, cache_k, cache_v):
    bs, n_new, _ = q_s.shape
    n_pages = page_table.shape[1]
    page = cache_k.shape[2]
    assert page == TILE and cache_k.shape[3:] == (N_HEADS, HEAD_DIM)
    assert n_new * N_HEADS <= TILE and n_new == SUBLANES
    pps = min(4, n_pages)
    assert n_pages % pps == 0
    nrow = N_HEADS * n_new
    wide = TILE * N_HEADS
    kv_spec = lambda r: pl.BlockSpec(
        (1, 1, page, N_HEADS, HEAD_DIM), lambda b, g, pt, r=r: (0, pt[b, g * pps + r], 0, 0, 0))
    const = lambda shape: pl.BlockSpec(shape, lambda b, g, pt: (0,) * len(shape))
    per_seq = lambda shape: pl.BlockSpec((1,) + shape, lambda b, g, pt: (b,) + (0,) * len(shape))
    grid_spec = pltpu.PrefetchScalarGridSpec(
        num_scalar_prefetch=1,
        grid=(bs, n_pages // pps),
        in_specs=[per_seq((n_new, ATTN_WIDTH)),
                  pl.BlockSpec((pps, n_new, TILE), lambda b, g, pt: (g, b, 0)),
                  pl.BlockSpec((1, n_new, TILE), lambda b, g, pt: (n_pages, b, 0)),
                  per_seq((n_new, N_HEADS, HEAD_DIM)), per_seq((n_new, N_HEADS, HEAD_DIM)),
                  const((TILE, wide)), const((2, nrow, wide)), const((nrow, TILE))]
                 + [kv_spec(r) for r in range(pps)] + [kv_spec(r) for r in range(pps)],
        out_specs=per_seq((nrow, HEAD_DIM)),
        scratch_shapes=[pltpu.VMEM((nrow, HEAD_DIM), BF16), pltpu.VMEM((nrow, LANES), F32),
                        pltpu.VMEM((nrow, LANES), F32), pltpu.VMEM((nrow, HEAD_DIM), F32)],
    )
    return pl.pallas_call(
        functools.partial(_sample_attn_body, pages_per_step=pps, n_pages=n_pages, n_new=n_new),
        grid_spec=grid_spec,
        out_shape=jax.ShapeDtypeStruct((bs, nrow, HEAD_DIM), F32),
        compiler_params=_params(("parallel", "arbitrary")),
        name="sample_attn",
    )(page_table, q_s, selb, selb, k_new, v_new, tables["expand"], tables["bias"], tables["bnew"],
      *([cache_k] * pps), *([cache_v] * pps))


def _t5_bucket(rel):
    n = np.maximum(np.asarray(rel, np.int64), 0)
    max_exact = REL_BUCKETS // 2
    nf = np.maximum(n, 1).astype(np.float32)
    scaled = (np.log(nf / np.float32(max_exact)) / np.float32(math.log(REL_MAX_DIST / max_exact))
              * np.float32(REL_BUCKETS - max_exact))
    large = np.minimum(max_exact + scaled.astype(np.int32), REL_BUCKETS - 1)
    return np.where(n < max_exact, n, large).astype(np.int32)


assert np.all(_t5_bucket(np.arange(TILE + 1, 1 << 18)) == REL_BUCKETS - 1)


def _bias_tables_body(rb_ref, b3_ref, bm_ref, bs_ref, bn_ref, o3_ref, om_ref, os_ref, on_ref, *, n_new):
    o3_ref[...] = jnp.full(o3_ref.shape, MASK_VALUE, F32)
    om_ref[...] = jnp.full(om_ref.shape, MASK_VALUE, F32)
    os_ref[...] = jnp.full(os_ref.shape, MASK_VALUE, F32)
    on_ref[...] = jnp.full(on_ref.shape, MASK_VALUE, F32)

    def body(bk, carry):
        hit3 = b3_ref[...] == bk
        hitm = bm_ref[...] == bk
        for h in range(N_HEADS):
            val = rb_ref[bk, h]
            rows = slice(h * n_new, (h + 1) * n_new)
            o3_ref[:, h] = jnp.where(hit3, val, o3_ref[:, h])
            om_ref[:, h] = jnp.where(hitm, val, om_ref[:, h])
            os_ref[:, rows, :] = jnp.where(bs_ref[:, rows, :] == bk, val, os_ref[:, rows, :])
            on_ref[rows, :] = jnp.where(bn_ref[rows, :] == bk, val, on_ref[rows, :])
        return carry

    lax.fori_loop(0, REL_BUCKETS, body, 0)


def _bias_tables(rel_bias, n_new):
    nrow = N_HEADS * n_new
    wide = TILE * N_HEADS
    i = np.arange(TILE)[:, None]
    jj = np.arange(TILE)[None, :]
    far = np.full((TILE, TILE), 2 * TILE)
    b3 = _t5_bucket(np.stack([i - jj, TILE + i - jj, far]))
    qi = np.arange(QB)[:, None]
    bm = _t5_bucket(np.stack([N_META + qi - jj, np.full((QB, TILE), 2 * TILE)]))
    rh = np.arange(nrow)[:, None] // n_new
    rq = np.arange(nrow)[:, None] % n_new
    cs = np.arange(wide)[None, :] // N_HEADS
    ch = np.arange(wide)[None, :] % N_HEADS
    bs = _t5_bucket(np.stack([np.full((nrow, wide), 2 * TILE), TILE + rq - cs]))
    bs = np.where(rh == ch, bs, -1)
    cj = np.arange(TILE)[None, :] // N_HEADS
    chn = np.arange(TILE)[None, :] % N_HEADS
    bn = np.where((rh == chn) & (cj < n_new), _t5_bucket(np.clip(rq - cj, 0, None)), -1)
    vm = lambda a: pl.BlockSpec(a.shape, lambda: (0,) * a.ndim)
    out_shape = (jax.ShapeDtypeStruct((3, N_HEADS, TILE, TILE), F32),
                 jax.ShapeDtypeStruct((2, N_HEADS, QB, TILE), F32),
                 jax.ShapeDtypeStruct((2, nrow, wide), F32),
                 jax.ShapeDtypeStruct((nrow, TILE), F32))
    consts = [jnp.asarray(a, I32) for a in (b3, bm, bs, bn)]
    return pl.pallas_call(
        functools.partial(_bias_tables_body, n_new=n_new),
        in_specs=[pl.BlockSpec(memory_space=pltpu.SMEM)] + [vm(a) for a in consts],
        out_specs=[vm(o) for o in out_shape],
        out_shape=out_shape,
        name="bias_tables",
    )(rel_bias.astype(F32), *consts)


def _expand_matrix():
    wide = TILE * N_HEADS
    e = np.zeros((TILE, wide), np.float32)
    e[np.arange(wide) // N_HEADS, np.arange(wide)] = 1.0
    return jnp.asarray(e, BF16)


def _split_w_in(w_in, d, c):
    sizes = (ATTN_WIDTH, ATTN_WIDTH, ATTN_WIDTH, IDX_WIDTH, IDX_DIM, IDX_HEADS, c, c, d, d)
    offs = np.concatenate([[0], np.cumsum(sizes)])
    assert offs[-1] == w_in.shape[1], (offs[-1], w_in.shape)
    names = ("q", "k", "v", "iq", "ik", "iw", "ca", "cb", "ga", "gb")
    parts = {n: w_in[:, offs[i]:offs[i + 1]].astype(BF16) for i, n in enumerate(names)}
    pad = jnp.zeros((d, LANES - IDX_DIM - IDX_HEADS), BF16)
    parts["ikw"] = jnp.concatenate([parts.pop("ik"), parts.pop("iw"), pad], axis=1)
    return parts


def _pad_rows(a, rows):
    return jnp.concatenate([a, jnp.zeros((rows - a.shape[0],) + a.shape[1:], a.dtype)], axis=0)


def kernel(x_prompt, x_sample, cache_k, cache_v, cache_ik, state_conv, page_table, meta_tokens, rel_bias,
           norm_mix_g, w_in, q_norm_g, k_norm_g, w_attn_br, w_dw, b_dw, ln_conv_g, ln_conv_b, w_conv_br,
           w_out, norm_ffn_g, w_ffn_gate, w_ffn_up, w_ffn_down):
    assert w_in.shape[0] == 1, "single trunk layer"
    b, seq, d = x_prompt.shape
    bs, n_new, _ = x_sample.shape
    c = w_dw.shape[2]
    n_pages = page_table.shape[1]
    past_len = n_pages * cache_k.shape[2]
    assert seq % QB == 0 and seq >= CONV_CTX and N_META + CONV_CTX <= 2 * HALO and N_META <= HALO
    t_real = seq + N_META

    w = _split_w_in(w_in[0], d, c)
    g_mix = norm_mix_g[0][None]
    g_ffn = norm_ffn_g[0][None]
    qg = q_norm_g[0][None]
    kg = k_norm_g[0][None]
    wa = w_attn_br[0].astype(BF16)
    wcb = w_conv_br[0].astype(BF16)
    wo = w_out[0].astype(BF16)
    wg = w_ffn_gate[0].astype(BF16)
    wu = w_ffn_up[0].astype(BF16)
    wd = w_ffn_down[0].astype(BF16)
    conv_w = (w_dw[0], b_dw[0][None], ln_conv_g[0][None], ln_conv_b[0][None])

    def tail(h2d, ao2d, yc2d):
        m = _gate(h2d, g_mix, ao2d, yc2d, w["ga"], w["gb"], wa, wcb)
        return _ffn(_outproj(h2d, m, wo), g_ffn, wg, wu, wd)

    _, kf_m, kb_m, vf_m, vb_m, _, u_m, ikw_m = _inproj(meta_tokens.astype(F32), g_mix, w, qg, kg)

    hp = x_prompt.reshape(b * seq, d)
    q, kf, kb, vf, vb, iq, u, ikw = _inproj(hp, g_mix, w, qg, kg)
    r3 = lambda a: a.reshape(b, seq, a.shape[-1])
    bias3, bmeta, sbias, bnew = _bias_tables(rel_bias, n_new)
    ao = _prompt_attn(r3(q), r3(iq), r3(ikw), r3(kb), r3(vb), _pad_rows(kb_m, TILE), _pad_rows(vb_m, TILE),
                      _pad_rows(ikw_m, TILE), bias3, bmeta, min(TOPK_MAX, seq // 4))
    halo_p = jnp.concatenate([jnp.zeros((HALO - N_META, c), F32), u_m], axis=0)[None]
    yc = _conv(r3(u), halo_p, *conv_w)
    y_prompt = tail(hp, ao.reshape(b * seq, ATTN_WIDTH), yc.reshape(b * seq, c)).reshape(b, seq, d)
    with_meta = lambda m_rows, rows: jnp.concatenate(
        [jnp.broadcast_to(m_rows[None], (b,) + m_rows.shape), r3(rows)], axis=1)
    k_prompt = with_meta(kf_m, kf).reshape(1, b, t_real, N_HEADS, HEAD_DIM)
    v_prompt = with_meta(vf_m, vf).reshape(1, b, t_real, N_HEADS, HEAD_DIM)
    ik_prompt = with_meta(ikw_m[:, :IDX_DIM], ikw[:, :IDX_DIM])[None]
    conv_prompt = r3(u)[None, :, seq - CONV_CTX:]

    hs = x_sample.reshape(bs * n_new, d)
    q, kf, kb, vf, vb, iq, u, ikw = _inproj(hs, g_mix, w, qg, kg)
    s3 = lambda a: a.reshape(bs, n_new, a.shape[-1])
    s4 = lambda a: a.reshape(bs, n_new, N_HEADS, HEAD_DIM)
    top_k = min(TOPK_MAX, (past_len + n_new) // 4)
    skey = _sample_scores(page_table, s3(iq), s3(ikw), cache_ik).reshape(n_pages + 1, bs * n_new, TILE)
    n_q = bs * n_new
    n_qp = -(-n_q // TILE) * TILE
    skey = jnp.pad(skey, ((0, 0), (0, n_qp - n_q), (0, 0)), constant_values=INT_MIN)
    selb = _sample_select(skey, top_k)
    ao_s = _sample_attn(page_table, s3(q), selb, s4(kf), s4(vf),
                        {"expand": _expand_matrix(), "bias": sbias, "bnew": bnew}, cache_k, cache_v)
    ao_s = ao_s.reshape(bs, N_HEADS, n_new, HEAD_DIM).transpose(0, 2, 1, 3).reshape(bs * n_new, ATTN_WIDTH)
    state = state_conv[0].astype(F32)
    halo_s = jnp.concatenate([jnp.zeros((bs, HALO - CONV_CTX, c), F32), state], axis=1)
    yc_s = _conv(s3(u), halo_s, *conv_w).reshape(bs * n_new, c)
    y_sample = tail(hs, ao_s.astype(BF16), yc_s).reshape(bs, n_new, d)
    k_sample = s4(kf)[None]
    v_sample = s4(vf)[None]
    ik_sample = s3(ikw)[None, :, :, :IDX_DIM]
    conv_sample = jnp.concatenate([state, s3(u)], axis=1)[None, :, n_new:]

    return (y_prompt, y_sample, k_prompt, v_prompt, ik_prompt, conv_prompt,
            k_sample, v_sample, ik_sample, conv_sample)
```
